```python
import math
import jax, jax.numpy as jnp
from jax import lax
import numpy as np

D_MODEL = 1024
BATCH = 4
SEQ = 8192
DEPTH = 4

CTX_LEN = 256
GRID_W = 64

DEEPNORM_ALPHA = (2.0 * DEPTH) ** 0.25
DEEPNORM_BETA = (8.0 * DEPTH) ** -0.25
LN_EPS = 1e-6

SSD_HEAD_DIM = 64
D_SSD = D_MODEL
SSD_HEADS = D_SSD // SSD_HEAD_DIM
SSD_GROUPS = 4
SSD_HPG = SSD_HEADS // SSD_GROUPS
SSD_STATE = 128
SSD_CHUNK = 128
CONV_W = 5
CONV_CH = D_SSD + 2 * SSD_GROUPS * SSD_STATE

POOL_GROUPS = 4
POOL_CH = D_MODEL // 8
POOL_WINDOWS = (2, 4, 8, 16)
POOL_OUT = D_MODEL // POOL_GROUPS

PEER_HEADS = 8
N_KEYS = 128
N_EXPERTS = N_KEYS * N_KEYS
PEER_TOPK = 16
PEER_QDIM = 256
D_HALF = PEER_QDIM // 2
PEER_BLOCK = 128

COL_B = D_SSD
COL_C = COL_B + SSD_GROUPS * SSD_STATE
COL_DTF = COL_C + SSD_GROUPS * SSD_STATE
COL_DTB = COL_DTF + SSD_HEADS
SCAN_COLS = COL_DTB + SSD_HEADS
COL_Z = SCAN_COLS
COL_POOL = COL_Z + D_SSD
COL_GP = COL_POOL + POOL_GROUPS * POOL_CH
COL_GS = COL_GP + D_MODEL
W_IN_COLS = COL_GS + D_MODEL

kernel_name = "hybrid_pool_ssd_peer_diffusion_block"


def layer_norm(x, g, b):
    xf = x.astype(jnp.float32)
    mu = jnp.mean(xf, axis=-1, keepdims=True)
    var = jnp.mean(jnp.square(xf - mu), axis=-1, keepdims=True)
    return ((xf - mu) * lax.rsqrt(var + LN_EPS) * g + b).astype(x.dtype)


def depthwise_conv(u, w, b):
    out = lax.conv_general_dilated(
        u, w[:, None, :], window_strides=(1,),
        padding=[(CONV_W // 2, CONV_W // 2)],
        dimension_numbers=("NWC", "WIO", "NWC"),
        feature_group_count=u.shape[-1])
    return out + b


def box_mean(u, w):
    L = u.shape[1]
    lo_off = w // 2
    hi_off = w - 1 - lo_off
    uf = u.astype(jnp.float32)
    cs = jnp.concatenate([jnp.zeros_like(uf[:, :1]), jnp.cumsum(uf, axis=1)], axis=1)
    t = jnp.arange(L)
    lo = jnp.maximum(t - lo_off, 0)
    hi = jnp.minimum(t + hi_off, L - 1)
    cnt = (hi - lo + 1).astype(jnp.float32).reshape((1, L) + (1,) * (u.ndim - 2))
    return ((cs[:, hi + 1] - cs[:, lo]) / cnt).astype(u.dtype)


def pool_branch(u, pool_w, pool_scale, on_grid):
    b, L, _ = u.shape
    ug = u.reshape(b, L, POOL_GROUPS, POOL_CH)
    outs = []
    for gi, w in enumerate(POOL_WINDOWS):
        ui = ug[:, :, gi]
        if on_grid:
            rows = L // GRID_W
            grid = box_mean(ui.reshape(b, rows, GRID_W, POOL_CH), w)
            grid = jnp.swapaxes(box_mean(jnp.swapaxes(grid, 1, 2), w), 1, 2)
            pooled = grid.reshape(b, L, POOL_CH)
        else:
            pooled = box_mean(ui, w)
        outs.append(pooled - ui)
    mixed = jnp.stack(outs, axis=2)
    y = jnp.einsum("blgc,gco->blgo", mixed, pool_w).reshape(b, L, D_MODEL)
    return y * pool_scale


def ssd_prepare(proj_scan, conv_w, conv_b):
    b, L, _ = proj_scan.shape
    xbc = jax.nn.silu(depthwise_conv(proj_scan[..., :COL_DTF], conv_w, conv_b))
    xh = xbc[..., :D_SSD].reshape(b, L, SSD_GROUPS, SSD_HPG, SSD_HEAD_DIM)
    bm = xbc[..., COL_B:COL_C].reshape(b, L, SSD_GROUPS, SSD_STATE)
    cm = xbc[..., COL_C:COL_DTF].reshape(b, L, SSD_GROUPS, SSD_STATE)
    dt_f = proj_scan[..., COL_DTF:COL_DTB].reshape(b, L, SSD_GROUPS, SSD_HPG)
    dt_b = proj_scan[..., COL_DTB:SCAN_COLS].reshape(b, L, SSD_GROUPS, SSD_HPG)
    return xh, bm, cm, dt_f, dt_b


def ssd_scan(xh, bm, cm, dt_raw, a_log, dt_bias, h0, reverse, with_output):
    if reverse:
        xh, bm, cm, dt_raw = [jnp.flip(t, 1) for t in (xh, bm, cm, dt_raw)]
    b, L = xh.shape[:2]
    nc = L // SSD_CHUNK
    dt = jax.nn.softplus(dt_raw + dt_bias.reshape(SSD_GROUPS, SSD_HPG))
    a = -jnp.exp(a_log.reshape(SSD_GROUPS, SSD_HPG))
    chunk = lambda t: t.reshape((b, nc, SSD_CHUNK) + t.shape[2:])
    xc, bc, cc, dtc = chunk(xh), chunk(bm), chunk(cm), chunk(dt)
    a_cum = jnp.cumsum(dtc * a, axis=2)
    decay_to_end = jnp.exp(a_cum[:, :, -1:] - a_cum)
    states = jnp.einsum("bcsgn,bcsgrp->bcgrpn", bc, (decay_to_end * dtc)[..., None] * xc)
    chunk_decay = jnp.exp(a_cum[:, :, -1])

    def step(h, inp):
        s, d = inp
        return d[..., None, None] * h + s, h

    h_last, h_in = lax.scan(step, h0, (jnp.moveaxis(states, 1, 0), jnp.moveaxis(chunk_decay, 1, 0)))
    if not with_output:
        return None, h_last
    h_in = jnp.moveaxis(h_in, 0, 1)
    seg = a_cum[:, :, :, None] - a_cum[:, :, None, :]
    lower = (jnp.arange(SSD_CHUNK)[:, None] >= jnp.arange(SSD_CHUNK)[None, :])[None, None, :, :, None, None]
    decay = jnp.exp(jnp.where(lower, seg, -jnp.inf))
    cb = jnp.einsum("bclgn,bcsgn->bclsg", cc, bc)
    wgt = cb[..., None] * decay * dtc[:, :, None]
    y = jnp.einsum("bclsgr,bcsgrp->bclgrp", wgt, xc)
    y = y + jnp.einsum("bclgn,bcgrpn->bclgrp", cc, h_in) * jnp.exp(a_cum)[..., None]
    y = y.reshape(b, L, SSD_GROUPS, SSD_HPG, SSD_HEAD_DIM)
    if reverse:
        y = jnp.flip(y, 1)
    return y, h_last


def bidir_ssd(xh, bm, cm, dt_f, dt_b, a_log, dt_bias, h0_f, h0_b, with_output):
    y_f, h_f = ssd_scan(xh, bm, cm, dt_f, a_log[0], dt_bias[0], h0_f, False, with_output)
    y_b, h_b = ssd_scan(xh, bm, cm, dt_b, a_log[1], dt_bias[1], h0_b, True, with_output)
    y = y_f + y_b if with_output else None
    return y, h_f, h_b


def ssd_finish(y, xh, z, d_skip, norm_g, w_ssd_out):
    b, L = y.shape[:2]
    y = y + d_skip.reshape(SSD_GROUPS, SSD_HPG)[..., None] * xh
    yg = (y.reshape(b, L, D_SSD) * jax.nn.silu(z)).astype(jnp.float32)
    yg = yg.reshape(b, L, SSD_GROUPS, D_SSD // SSD_GROUPS)
    yg = yg * lax.rsqrt(jnp.mean(yg * yg, axis=-1, keepdims=True) + LN_EPS)
    y = (yg.reshape(b, L, D_SSD) * norm_g).astype(z.dtype)
    return y @ w_ssd_out


def mixer_out(proj, y_ssd, xh, d_skip, norm_g, w_ssd_out, pool_w, pool_scale, w_o, on_grid):
    y_pool = pool_branch(proj[..., COL_POOL:COL_GP], pool_w, pool_scale, on_grid)
    y_s = ssd_finish(y_ssd, xh, proj[..., COL_Z:COL_POOL], d_skip, norm_g, w_ssd_out)
    merged = (jax.nn.sigmoid(proj[..., COL_GP:COL_GS]) * y_pool
              + jax.nn.sigmoid(proj[..., COL_GS:W_IN_COLS]) * y_s)
    return merged @ w_o


def peer(h, w_q, sub_keys, u_tab, v_tab):
    b, L, _ = h.shape
    q = (h @ w_q).reshape(b, L, PEER_HEADS, 2, D_HALF)
    scores = jnp.einsum("blhtd,htkd->blhtk", q, sub_keys)
    sv, si = lax.top_k(scores, PEER_TOPK)
    n_cand = PEER_TOPK * PEER_TOPK
    cand_s = (sv[..., 0, :, None] + sv[..., 1, None, :]).reshape(b, L, PEER_HEADS, n_cand)
    cand_i = (si[..., 0, :, None] * N_KEYS + si[..., 1, None, :]).reshape(b, L, PEER_HEADS, n_cand)
    top_s, pos = lax.top_k(cand_s, PEER_TOPK)
    idx = jnp.take_along_axis(cand_i, pos, axis=-1)
    gate = jax.nn.softmax(top_s.astype(jnp.float32), axis=-1).astype(h.dtype)
    n_tok = b * L
    n_sel = PEER_HEADS * PEER_TOPK
    nb = n_tok // PEER_BLOCK
    xb = h.reshape(nb, PEER_BLOCK, D_MODEL)
    ib = idx.reshape(nb, PEER_BLOCK, n_sel)
    gb = gate.reshape(nb, PEER_BLOCK, n_sel)

    def experts(args):
        xt, it, gt = args
        act = jax.nn.gelu(jnp.einsum("tkd,td->tk", u_tab[it], xt))
        return jnp.einsum("tk,tkd->td", gt * act, v_tab[it])

    out = lax.map(experts, (xb, ib, gb))
    return out.reshape(b, L, D_MODEL)


def setup_inputs(seed: int = 0) -> dict:
    key = jax.random.key(seed)
    ks = jax.random.split(key, 24)
    D = D_MODEL
    nrm = lambda k, shape, s: jax.random.normal(k, shape, jnp.float32) * s
    x = nrm(ks[0], (BATCH, SEQ, D), 1.0)
    c = nrm(ks[1], (BATCH, D), 1.0)
    ctx = nrm(ks[2], (BATCH, CTX_LEN, D), 1.0)
    c_ctx = nrm(ks[3], (D,), 1.0)
    w_mod = nrm(ks[4], (DEPTH, D, 6 * D), 0.5 * D ** -0.5)
    b_mod = nrm(ks[5], (DEPTH, 6 * D), 0.01)
    w_in = nrm(ks[6], (DEPTH, D, W_IN_COLS), D ** -0.5)
    conv_w = nrm(ks[7], (DEPTH, CONV_W, CONV_CH), CONV_W ** -0.5)
    conv_b = nrm(ks[8], (DEPTH, CONV_CH), 0.01)
    a_log = jnp.log(jax.random.uniform(ks[9], (DEPTH, 2, SSD_HEADS), jnp.float32, 1.0, 16.0))
    dt0 = jnp.exp(jax.random.uniform(ks[10], (DEPTH, 2, SSD_HEADS), jnp.float32,
                                     math.log(1e-3), math.log(1e-1)))
    dt_bias = dt0 + jnp.log(-jnp.expm1(-dt0))
    d_skip = 1.0 + nrm(ks[11], (DEPTH, SSD_HEADS), 0.1)
    ssd_norm_g = 1.0 + nrm(ks[12], (DEPTH, D_SSD), 0.1)
    w_ssd_out = nrm(ks[13], (DEPTH, D_SSD, D), D_SSD ** -0.5)
    pool_w = nrm(ks[14], (DEPTH, POOL_GROUPS, POOL_CH, POOL_OUT), POOL_CH ** -0.5)
    pool_scale = 1.0 + nrm(ks[15], (DEPTH, D), 0.1)
    w_o = nrm(ks[16], (DEPTH, D, D), DEEPNORM_BETA * D ** -0.5)
    ln_g = 1.0 + nrm(ks[17], (DEPTH, 2, D), 0.1)
    ln_b = nrm(ks[18], (DEPTH, 2, D), 0.01)
    w_q = nrm(ks[19], (DEPTH, D, PEER_HEADS * PEER_QDIM), D ** -0.5)
    sub_keys = nrm(ks[20], (DEPTH, PEER_HEADS, 2, N_KEYS, D_HALF), D_HALF ** -0.5)
    u_tab = nrm(ks[21], (DEPTH, N_EXPERTS, D), D ** -0.5)
    v_tab = nrm(ks[22], (DEPTH, N_EXPERTS, D), DEEPNORM_BETA)
    return {"x": x, "c": c, "ctx": ctx, "c_ctx": c_ctx, "w_mod": w_mod, "b_mod": b_mod,
            "w_in": w_in, "conv_w": conv_w, "conv_b": conv_b, "a_log": a_log, "dt_bias": dt_bias,
            "d_skip": d_skip, "ssd_norm_g": ssd_norm_g, "w_ssd_out": w_ssd_out, "pool_w": pool_w,
            "pool_scale": pool_scale, "w_o": w_o, "ln_g": ln_g, "ln_b": ln_b, "w_q": w_q,
            "sub_keys": sub_keys, "u_tab": u_tab, "v_tab": v_tab}


def reference(x, c, ctx, c_ctx, w_mod, b_mod, w_in, conv_w, conv_b, a_log, dt_bias, d_skip,
              ssd_norm_g, w_ssd_out, pool_w, pool_scale, w_o, ln_g, ln_b, w_q, sub_keys, u_tab, v_tab):
    D = D_MODEL
    silu_c = jax.nn.silu(c)
    silu_cc = jax.nn.silu(c_ctx)
    for l in range(DEPTH):
        last = l == DEPTH - 1
        mod = (silu_c @ w_mod[l] + b_mod[l])[:, None, :]
        sh1, sc1, g1, sh2, sc2, g2 = jnp.split(mod, 6, axis=-1)
        n_ctx_cols = 2 * D if last else 6 * D
        mod_c = silu_cc @ w_mod[l][:, :n_ctx_cols] + b_mod[l][:n_ctx_cols]
        mods_c = jnp.split(mod_c, n_ctx_cols // D)

        hc = ctx * (1 + mods_c[1]) + mods_c[0]
        proj_c = hc @ (w_in[l][:, :SCAN_COLS] if last else w_in[l])
        xh_c, b_c, c_c, dtf_c, dtb_c = ssd_prepare(proj_c[..., :SCAN_COLS], conv_w[l], conv_b[l])
        h0 = jnp.zeros((ctx.shape[0], SSD_GROUPS, SSD_HPG, SSD_HEAD_DIM, SSD_STATE), ctx.dtype)
        y_c, hf_c, hb_c = bidir_ssd(xh_c, b_c, c_c, dtf_c, dtb_c, a_log[l], dt_bias[l], h0, h0, not last)

        h = x * (1 + sc1) + sh1
        proj = h @ w_in[l]
        xh, bm, cm, dtf, dtb = ssd_prepare(proj[..., :SCAN_COLS], conv_w[l], conv_b[l])
        y, _, _ = bidir_ssd(xh, bm, cm, dtf, dtb, a_log[l], dt_bias[l], hf_c, hb_c, True)
        mix = mixer_out(proj, y, xh, d_skip[l], ssd_norm_g[l], w_ssd_out[l], pool_w[l], pool_scale[l],
                        w_o[l], True)
        x = layer_norm(DEEPNORM_ALPHA * x + g1 * mix, ln_g[l, 0], ln_b[l, 0])

        h2 = x * (1 + sc2) + sh2
        x = layer_norm(DEEPNORM_ALPHA * x + g2 * peer(h2, w_q[l], sub_keys[l], u_tab[l], v_tab[l]),
                       ln_g[l, 1], ln_b[l, 1])

        if not last:
            mix_c = mixer_out(proj_c, y_c, xh_c, d_skip[l], ssd_norm_g[l], w_ssd_out[l], pool_w[l],
                              pool_scale[l], w_o[l], False)
            ctx = layer_norm(DEEPNORM_ALPHA * ctx + mods_c[2] * mix_c, ln_g[l, 0], ln_b[l, 0])
            h2c = ctx * (1 + mods_c[4]) + mods_c[3]
            ctx = layer_norm(DEEPNORM_ALPHA * ctx
                             + mods_c[5] * peer(h2c, w_q[l], sub_keys[l], u_tab[l], v_tab[l]),
                             ln_g[l, 1], ln_b[l, 1])
    return x
```

```python
import functools
import math

import jax
import jax.numpy as jnp
from jax import lax
from jax.experimental import pallas as pl
from jax.experimental.pallas import tpu as pltpu

D_MODEL = 1024
DEPTH = 4
GRID_W = 64
DEEPNORM_ALPHA = (2.0 * DEPTH) ** 0.25
LN_EPS = 1e-6

SSD_HEAD_DIM = 64
D_SSD = D_MODEL
SSD_HEADS = D_SSD // SSD_HEAD_DIM
SSD_GROUPS = 4
SSD_HPG = SSD_HEADS // SSD_GROUPS
SSD_STATE = 128
SSD_CHUNK = 128
CONV_W = 5

POOL_GROUPS = 4
POOL_CH = D_MODEL // 8
POOL_WINDOWS = (2, 4, 8, 16)

PEER_HEADS = 8
N_KEYS = 128
N_EXPERTS = N_KEYS * N_KEYS
PEER_TOPK = 16
PEER_QDIM = 256
D_HALF = PEER_QDIM // 2

COL_B = D_SSD
COL_C = COL_B + SSD_GROUPS * SSD_STATE
COL_DTF = COL_C + SSD_GROUPS * SSD_STATE
COL_DTB = COL_DTF + SSD_HEADS
SCAN_COLS = COL_DTB + SSD_HEADS
COL_Z = SCAN_COLS
COL_POOL = COL_Z + D_SSD
COL_GP = COL_POOL + POOL_GROUPS * POOL_CH
COL_GS = COL_GP + D_MODEL
W_IN_COLS = COL_GS + D_MODEL

F32 = jnp.float32
BF16 = jnp.bfloat16

VMEM_LIMIT_BYTES = 56 * 1024 * 1024


def _mod_matmul_kernel(a_ref, sc_ref, sh_ref, w_ref, o_ref):
    a = a_ref[0] * (1.0 + sc_ref[0]) + sh_ref[0]
    o_ref[0] = jnp.dot(a.astype(BF16), w_ref[...], preferred_element_type=F32)


def mod_matmul(a, scale, shift, w, tm=512, tn=None):
    b, L, K = a.shape
    N = w.shape[1]
    if tn is None:
        tn = N
    assert L % tm == 0 and N % tn == 0
    return pl.pallas_call(
        _mod_matmul_kernel,
        grid=(b, L // tm, N // tn),
        in_specs=[
            pl.BlockSpec((1, tm, K), lambda i, j, k: (i, j, 0)),
            pl.BlockSpec((1, 1, K), lambda i, j, k: (i, 0, 0)),
            pl.BlockSpec((1, 1, K), lambda i, j, k: (i, 0, 0)),
            pl.BlockSpec((K, tn), lambda i, j, k: (0, k)),
        ],
        out_specs=pl.BlockSpec((1, tm, tn), lambda i, j, k: (i, j, k)),
        out_shape=jax.ShapeDtypeStruct((b, L, N), F32),
        compiler_params=pltpu.CompilerParams(
            dimension_semantics=("parallel", "parallel", "arbitrary"),
            vmem_limit_bytes=VMEM_LIMIT_BYTES),
        name="mod_matmul",
    )(a, scale, shift, w)


def _matmul_kernel(a_ref, w_ref, o_ref):
    o_ref[...] = jnp.dot(a_ref[...].astype(BF16), w_ref[...], preferred_element_type=F32)


def matmul(a, w, tm=512, tn=None):
    M, K = a.shape
    N = w.shape[1]
    if tn is None:
        tn = N
    tm = min(tm, M)
    assert M % tm == 0 and N % tn == 0
    return pl.pallas_call(
        _matmul_kernel,
        grid=(M // tm, N // tn),
        in_specs=[
            pl.BlockSpec((tm, K), lambda i, k: (i, 0)),
            pl.BlockSpec((K, tn), lambda i, k: (0, k)),
        ],
        out_specs=pl.BlockSpec((tm, tn), lambda i, k: (i, k)),
        out_shape=jax.ShapeDtypeStruct((M, N), F32),
        compiler_params=pltpu.CompilerParams(
            dimension_semantics=("parallel", "arbitrary"),
            vmem_limit_bytes=VMEM_LIMIT_BYTES),
        name="matmul",
    )(a, w)


PEER_TOK = 512
PEER_EC = 1024
PEER_ROWS = 32


def _gelu_tanh(x):
    return 0.5 * x * (1.0 + jnp.tanh(0.7978845608028654 * (x + 0.044715 * (x * x * x))))


def _peer_kernel(xT_ref, u_ref, vT_ref, s1_ref, s2_ref, e1_ref, e2_ref, tau_ref, o_ref,
                 s_scr, w_scr):
    c = pl.program_id(1)

    @pl.when(c == 0)
    def _():
        o_ref[...] = jnp.zeros_like(o_ref)

    s_scr[...] = jnp.dot(u_ref[...], xT_ref[...], preferred_element_type=F32)

    n_i = PEER_EC // N_KEYS
    n_rg = N_KEYS // PEER_ROWS

    def body(it, carry):
        il = it // n_rg
        rg = it % n_rg
        i = c * n_i + il
        r0 = pl.multiple_of(rg * PEER_ROWS, PEER_ROWS)
        g = jnp.zeros((PEER_ROWS, PEER_TOK), F32)
        for h in range(PEER_HEADS):
            s1row = s1_ref[h, pl.ds(i, 1), :]
            e1row = e1_ref[h, pl.ds(i, 1), :]
            tau = tau_ref[pl.ds(h, 1), :]
            sm = s2_ref[h, pl.ds(r0, PEER_ROWS), :] + s1row
            p = e2_ref[h, pl.ds(r0, PEER_ROWS), :] * e1row
            g = g + jnp.where(sm >= tau, p, 0.0)
        row0 = pl.multiple_of(il * N_KEYS + r0, PEER_ROWS)
        act = _gelu_tanh(s_scr[pl.ds(row0, PEER_ROWS), :])
        w_scr[pl.ds(row0, PEER_ROWS), :] = (g * act).astype(BF16)
        return carry

    lax.fori_loop(0, n_i * n_rg, body, 0)
    o_ref[...] += jnp.dot(vT_ref[...], w_scr[...], preferred_element_type=F32)


def peer_dense(xT, u_bf, vT_bf, s1T, s2T, e1T, e2T, tauT):
    D, N = xT.shape
    assert N % PEER_TOK == 0
    H = PEER_HEADS
    return pl.pallas_call(
        _peer_kernel,
        grid=(N // PEER_TOK, N_EXPERTS // PEER_EC),
        in_specs=[
            pl.BlockSpec((D, PEER_TOK), lambda t, c: (0, t)),
            pl.BlockSpec((PEER_EC, D), lambda t, c: (c, 0)),
            pl.BlockSpec((D, PEER_EC), lambda t, c: (0, c)),
            pl.BlockSpec((H, N_KEYS, PEER_TOK), lambda t, c: (0, 0, t)),
            pl.BlockSpec((H, N_KEYS, PEER_TOK), lambda t, c: (0, 0, t)),
            pl.BlockSpec((H, N_KEYS, PEER_TOK), lambda t, c: (0, 0, t)),
            pl.BlockSpec((H, N_KEYS, PEER_TOK), lambda t, c: (0, 0, t)),
            pl.BlockSpec((H, PEER_TOK), lambda t, c: (0, t)),
        ],
        out_specs=pl.BlockSpec((D, PEER_TOK), lambda t, c: (0, t)),
        out_shape=jax.ShapeDtypeStruct((D, N), F32),
        scratch_shapes=[
            pltpu.VMEM((PEER_EC, PEER_TOK), F32),
            pltpu.VMEM((PEER_EC, PEER_TOK), BF16),
        ],
        compiler_params=pltpu.CompilerParams(
            dimension_semantics=("parallel", "arbitrary"),
            vmem_limit_bytes=VMEM_LIMIT_BYTES),
        name="peer_dense",
    )(xT, u_bf, vT_bf, s1T, s2T, e1T, e2T, tauT)


def peer(h2, w_q_bf, sub_keys, u_bf, vT_bf):
    N = h2.shape[0]
    q = matmul(h2, w_q_bf, tn=1024).reshape(N, PEER_HEADS, 2, D_HALF)
    scores = jnp.einsum("nhtd,htkd->nhtk", q, sub_keys)
    sv, _ = lax.top_k(scores, PEER_TOPK)
    cand_s = (sv[..., 0, :, None] + sv[..., 1, None, :]).reshape(N, PEER_HEADS, PEER_TOPK * PEER_TOPK)
    top_s, _ = lax.top_k(cand_s, PEER_TOPK)
    tau = top_s[..., -1]
    zsum = jnp.sum(jnp.exp(top_s - top_s[..., :1]), axis=-1)
    s1 = scores[:, :, 0, :]
    s2 = scores[:, :, 1, :]
    e1 = jnp.exp(s1 - sv[:, :, 0, :1])
    e2 = jnp.exp(s2 - sv[:, :, 1, :1]) / zsum[..., None]
    tr = lambda t: jnp.transpose(t, (1, 2, 0))
    outT = peer_dense(h2.T.astype(BF16), u_bf, vT_bf, tr(s1), tr(s2), tr(e1), tr(e2), tau.T)
    return outT.T


def layer_norm(x, g, b):
    mu = jnp.mean(x, axis=-1, keepdims=True)
    var = jnp.mean(jnp.square(x - mu), axis=-1, keepdims=True)
    return (x - mu) * lax.rsqrt(var + LN_EPS) * g + b


def depthwise_conv(u, w, b):
    out = lax.conv_general_dilated(
        u, w[:, None, :], window_strides=(1,),
        padding=[(CONV_W // 2, CONV_W // 2)],
        dimension_numbers=("NWC", "WIO", "NWC"),
        feature_group_count=u.shape[-1])
    return out + b


def box_mean(u, w):
    L = u.shape[1]
    lo_off = w // 2
    hi_off = w - 1 - lo_off
    cs = jnp.concatenate([jnp.zeros_like(u[:, :1]), jnp.cumsum(u, axis=1)], axis=1)
    t = jnp.arange(L)
    lo = jnp.maximum(t - lo_off, 0)
    hi = jnp.minimum(t + hi_off, L - 1)
    cnt = (hi - lo + 1).astype(F32).reshape((1, L) + (1,) * (u.ndim - 2))
    return (cs[:, hi + 1] - cs[:, lo]) / cnt


def pool_branch(u, pool_w, pool_scale, on_grid):
    b, L, _ = u.shape
    ug = u.reshape(b, L, POOL_GROUPS, POOL_CH)
    outs = []
    for gi, w in enumerate(POOL_WINDOWS):
        ui = ug[:, :, gi]
        if on_grid:
            rows = L // GRID_W
            grid = box_mean(ui.reshape(b, rows, GRID_W, POOL_CH), w)
            grid = jnp.swapaxes(box_mean(jnp.swapaxes(grid, 1, 2), w), 1, 2)
            pooled = grid.reshape(b, L, POOL_CH)
        else:
            pooled = box_mean(ui, w)
        outs.append(pooled - ui)
    mixed = jnp.stack(outs, axis=2)
    y = jnp.einsum("blgc,gco->blgo", mixed, pool_w).reshape(b, L, D_MODEL)
    return y * pool_scale


def ssd_prepare(proj_scan, conv_w, conv_b):
    b, L, _ = proj_scan.shape
    xbc = jax.nn.silu(depthwise_conv(proj_scan[..., :COL_DTF], conv_w, conv_b))
    xh = xbc[..., :D_SSD].reshape(b, L, SSD_GROUPS, SSD_HPG, SSD_HEAD_DIM)
    bm = xbc[..., COL_B:COL_C].reshape(b, L, SSD_GROUPS, SSD_STATE)
    cm = xbc[..., COL_C:COL_DTF].reshape(b, L, SSD_GROUPS, SSD_STATE)
    dt_f = proj_scan[..., COL_DTF:COL_DTB].reshape(b, L, SSD_GROUPS, SSD_HPG)
    dt_b = proj_scan[..., COL_DTB:SCAN_COLS].reshape(b, L, SSD_GROUPS, SSD_HPG)
    return xh, bm, cm, dt_f, dt_b


def ssd_scan(xh, bm, cm, dt_raw, a_log, dt_bias, h0, reverse, with_output):
    if reverse:
        xh, bm, cm, dt_raw = [jnp.flip(t, 1) for t in (xh, bm, cm, dt_raw)]
    b, L = xh.shape[:2]
    nc = L // SSD_CHUNK
    dt = jax.nn.softplus(dt_raw + dt_bias.reshape(SSD_GROUPS, SSD_HPG))
    a = -jnp.exp(a_log.reshape(SSD_GROUPS, SSD_HPG))
    chunk = lambda t: t.reshape((b, nc, SSD_CHUNK) + t.shape[2:])
    xc, bc, cc, dtc = chunk(xh), chunk(bm), chunk(cm), chunk(dt)
    a_cum = jnp.cumsum(dtc * a, axis=2)
    decay_to_end = jnp.exp(a_cum[:, :, -1:] - a_cum)
    states = jnp.einsum("bcsgn,bcsgrp->bcgrpn", bc, (decay_to_end * dtc)[..., None] * xc)
    chunk_decay = jnp.exp(a_cum[:, :, -1])

    def step(h, inp):
        s, d = inp
        return d[..., None, None] * h + s, h

    h_last, h_in = lax.scan(step, h0, (jnp.moveaxis(states, 1, 0), jnp.moveaxis(chunk_decay, 1, 0)))
    if not with_output:
        return None, h_last
    h_in = jnp.moveaxis(h_in, 0, 1)
    seg = a_cum[:, :, :, None] - a_cum[:, :, None, :]
    lower = (jnp.arange(SSD_CHUNK)[:, None] >= jnp.arange(SSD_CHUNK)[None, :])[None, None, :, :, None, None]
    decay = jnp.exp(jnp.where(lower, seg, -jnp.inf))
    cb = jnp.einsum("bclgn,bcsgn->bclsg", cc, bc)
    wgt = cb[..., None] * decay * dtc[:, :, None]
    y = jnp.einsum("bclsgr,bcsgrp->bclgrp", wgt, xc)
    y = y + jnp.einsum("bclgn,bcgrpn->bclgrp", cc, h_in) * jnp.exp(a_cum)[..., None]
    y = y.reshape(b, L, SSD_GROUPS, SSD_HPG, SSD_HEAD_DIM)
    if reverse:
        y = jnp.flip(y, 1)
    return y, h_last


def bidir_ssd(xh, bm, cm, dt_f, dt_b, a_log, dt_bias, h0_f, h0_b, with_output):
    y_f, h_f = ssd_scan(xh, bm, cm, dt_f, a_log[0], dt_bias[0], h0_f, False, with_output)
    y_b, h_b = ssd_scan(xh, bm, cm, dt_b, a_log[1], dt_bias[1], h0_b, True, with_output)
    y = y_f + y_b if with_output else None
    return y, h_f, h_b


def mixer_out(proj, y, xh, d_skip, norm_g, w_ssd_out_bf, pool_w, pool_scale, w_o_bf, on_grid):
    b, L = y.shape[:2]
    y_pool = pool_branch(proj[..., COL_POOL:COL_GP], pool_w, pool_scale, on_grid)
    z = proj[..., COL_Z:COL_POOL]
    y = y + d_skip.reshape(SSD_GROUPS, SSD_HPG)[..., None] * xh
    yg = y.reshape(b, L, D_SSD) * jax.nn.silu(z)
    yg = yg.reshape(b, L, SSD_GROUPS, D_SSD // SSD_GROUPS)
    yg = yg * lax.rsqrt(jnp.mean(yg * yg, axis=-1, keepdims=True) + LN_EPS)
    yn = yg.reshape(b * L, D_SSD) * norm_g
    y_s = matmul(yn, w_ssd_out_bf).reshape(b, L, D_MODEL)
    merged = (jax.nn.sigmoid(proj[..., COL_GP:COL_GS]) * y_pool
              + jax.nn.sigmoid(proj[..., COL_GS:W_IN_COLS]) * y_s)
    return matmul(merged.reshape(b * L, D_MODEL), w_o_bf).reshape(b, L, D_MODEL)


def kernel(x, c, ctx, c_ctx, w_mod, b_mod, w_in, conv_w, conv_b, a_log, dt_bias, d_skip,
           ssd_norm_g, w_ssd_out, pool_w, pool_scale, w_o, ln_g, ln_b, w_q, sub_keys, u_tab, v_tab):
    D = D_MODEL
    B, L, _ = x.shape
    LC = ctx.shape[1]
    silu_c = jax.nn.silu(c)
    silu_cc = jax.nn.silu(c_ctx)
    ones_b = jnp.ones((B, 1, 1), F32)
    for l in range(DEPTH):
        last = l == DEPTH - 1
        w_mod_bf = w_mod[l].astype(BF16)
        mod_in = jnp.concatenate([silu_c, silu_cc[None]], axis=0)
        mod_all = matmul(jnp.pad(mod_in, ((0, 8 - (B + 1)), (0, 0))), w_mod_bf, tn=1536) + b_mod[l]
        mod = mod_all[:B, None, :]
        sh1, sc1, g1, sh2, sc2, g2 = jnp.split(mod, 6, axis=-1)
        mods_c = jnp.split(mod_all[B], 6)

        w_in_bf = w_in[l].astype(BF16)
        w_in_pad = jnp.pad(w_in_bf, ((0, 0), (0, 5760 - W_IN_COLS)))
        u_bf = u_tab[l].astype(BF16)
        vT_bf = v_tab[l].T.astype(BF16)
        w_q_bf = w_q[l].astype(BF16)
        w_ssd_out_bf = w_ssd_out[l].astype(BF16)
        w_o_bf = w_o[l].astype(BF16)

        sc_c = mods_c[1][None, None, :] * ones_b
        sh_c = mods_c[0][None, None, :] * ones_b
        proj_c = mod_matmul(ctx, sc_c, sh_c, w_in_pad, tm=LC, tn=1152)[..., :W_IN_COLS]
        xh_c, b_c, c_c, dtf_c, dtb_c = ssd_prepare(proj_c[..., :SCAN_COLS], conv_w[l], conv_b[l])
        h0 = jnp.zeros((B, SSD_GROUPS, SSD_HPG, SSD_HEAD_DIM, SSD_STATE), F32)
        y_c, hf_c, hb_c = bidir_ssd(xh_c, b_c, c_c, dtf_c, dtb_c, a_log[l], dt_bias[l], h0, h0, not last)

        proj = mod_matmul(x, sc1, sh1, w_in_pad, tm=512, tn=1152)[..., :W_IN_COLS]
        xh, bm, cm, dtf, dtb = ssd_prepare(proj[..., :SCAN_COLS], conv_w[l], conv_b[l])
        y, _, _ = bidir_ssd(xh, bm, cm, dtf, dtb, a_log[l], dt_bias[l], hf_c, hb_c, True)
        mix = mixer_out(proj, y, xh, d_skip[l], ssd_norm_g[l], w_ssd_out_bf, pool_w[l], pool_scale[l],
                        w_o_bf, True)
        x = layer_norm(DEEPNORM_ALPHA * x + g1 * mix, ln_g[l, 0], ln_b[l, 0])

        if not last:
            mix_c = mixer_out(proj_c, y_c, xh_c, d_skip[l], ssd_norm_g[l], w_ssd_out_bf, pool_w[l],
                              pool_scale[l], w_o_bf, False)
            ctx = layer_norm(DEEPNORM_ALPHA * ctx + mods_c[2] * mix_c, ln_g[l, 0], ln_b[l, 0])

        h2 = (x * (1 + sc2) + sh2).reshape(B * L, D)
        if not last:
            h2c = (ctx * (1 + mods_c[4]) + mods_c[3]).reshape(B * LC, D)
            h2 = jnp.concatenate([h2, h2c], axis=0)
        pe = peer(h2, w_q_bf, sub_keys[l], u_bf, vT_bf)
        x = layer_norm(DEEPNORM_ALPHA * x + g2 * pe[:B * L].reshape(B, L, D), ln_g[l, 1], ln_b[l, 1])
        if not last:
            ctx = layer_norm(DEEPNORM_ALPHA * ctx + mods_c[5] * pe[B * L:].reshape(B, LC, D),
                             ln_g[l, 1], ln_b[l, 1])
    return x
```

```python
import functools
import math

import jax
import jax.numpy as jnp
from jax import lax
from jax.experimental import pallas as pl
from jax.experimental.pallas import tpu as pltpu

D_MODEL = 1024
DEPTH = 4
GRID_W = 64
DEEPNORM_ALPHA = (2.0 * DEPTH) ** 0.25
LN_EPS = 1e-6

SSD_HEAD_DIM = 64
D_SSD = D_MODEL
SSD_HEADS = D_SSD // SSD_HEAD_DIM
SSD_GROUPS = 4
SSD_HPG = SSD_HEADS // SSD_GROUPS
SSD_STATE = 128
SSD_CHUNK = 128
CONV_W = 5

POOL_GROUPS = 4
POOL_CH = D_MODEL // 8
POOL_WINDOWS = (2, 4, 8, 16)

PEER_HEADS = 8
N_KEYS = 128
N_EXPERTS = N_KEYS * N_KEYS
PEER_TOPK = 16
PEER_QDIM = 256
D_HALF = PEER_QDIM // 2

COL_B = D_SSD
COL_C = COL_B + SSD_GROUPS * SSD_STATE
COL_DTF = COL_C + SSD_GROUPS * SSD_STATE
COL_DTB = COL_DTF + SSD_HEADS
SCAN_COLS = COL_DTB + SSD_HEADS
COL_Z = SCAN_COLS
COL_POOL = COL_Z + D_SSD
COL_GP = COL_POOL + POOL_GROUPS * POOL_CH
COL_GS = COL_GP + D_MODEL
W_IN_COLS = COL_GS + D_MODEL

F32 = jnp.float32
BF16 = jnp.bfloat16

VMEM_LIMIT_BYTES = 56 * 1024 * 1024


def _mod_matmul_kernel(a_ref, sc_ref, sh_ref, w_ref, o_ref):
    a = a_ref[0] * (1.0 + sc_ref[0]) + sh_ref[0]
    o_ref[0] = jnp.dot(a.astype(BF16), w_ref[...], preferred_element_type=F32)


def mod_matmul(a, scale, shift, w, tm=512, tn=None):
    b, L, K = a.shape
    N = w.shape[1]
    if tn is None:
        tn = N
    assert L % tm == 0 and N % tn == 0
    return pl.pallas_call(
        _mod_matmul_kernel,
        grid=(b, L // tm, N // tn),
        in_specs=[
            pl.BlockSpec((1, tm, K), lambda i, j, k: (i, j, 0)),
            pl.BlockSpec((1, 1, K), lambda i, j, k: (i, 0, 0)),
            pl.BlockSpec((1, 1, K), lambda i, j, k: (i, 0, 0)),
            pl.BlockSpec((K, tn), lambda i, j, k: (0, k)),
        ],
        out_specs=pl.BlockSpec((1, tm, tn), lambda i, j, k: (i, j, k)),
        out_shape=jax.ShapeDtypeStruct((b, L, N), F32),
        compiler_params=pltpu.CompilerParams(
            dimension_semantics=("parallel", "parallel", "arbitrary"),
            vmem_limit_bytes=VMEM_LIMIT_BYTES),
        name="mod_matmul",
    )(a, scale, shift, w)


def _matmul_kernel(a_ref, w_ref, o_ref):
    o_ref[...] = jnp.dot(a_ref[...].astype(BF16), w_ref[...], preferred_element_type=F32)


def matmul(a, w, tm=512, tn=None):
    M, K = a.shape
    N = w.shape[1]
    if tn is None:
        tn = N
    tm = min(tm, M)
    assert M % tm == 0 and N % tn == 0
    return pl.pallas_call(
        _matmul_kernel,
        grid=(M // tm, N // tn),
        in_specs=[
            pl.BlockSpec((tm, K), lambda i, k: (i, 0)),
            pl.BlockSpec((K, tn), lambda i, k: (0, k)),
        ],
        out_specs=pl.BlockSpec((tm, tn), lambda i, k: (i, k)),
        out_shape=jax.ShapeDtypeStruct((M, N), F32),
        compiler_params=pltpu.CompilerParams(
            dimension_semantics=("parallel", "arbitrary"),
            vmem_limit_bytes=VMEM_LIMIT_BYTES),
        name="matmul",
    )(a, w)


SEL_TOK = 256
NOT_RANKED = 127.0


def _dup_bf16_words(x):
    hi = lax.bitcast_convert_type(x.astype(BF16).astype(F32), jnp.uint32)
    return hi | (hi >> 16)


def _top16(s, vals_ref):
    rank = jnp.full(s.shape, NOT_RANKED, F32)
    for r in range(PEER_TOPK):
        m = jnp.max(s, axis=0, keepdims=True)
        eq = s == m
        rank = jnp.where(eq, float(r), rank)
        s = jnp.where(eq, -jnp.inf, s)
        vals_ref[r:r + 1, :] = m
    return rank


def _peer_select_kernel(hT_ref, wqT_ref, keys_ref, l1w_ref, e1w_ref, r2_ref, e2_ref,
                        q_scr, v1_scr, v2_scr):
    q_scr[...] = jnp.dot(wqT_ref[...], hT_ref[...], preferred_element_type=F32).astype(BF16)

    def head(h, carry):
        off = pl.multiple_of(h * PEER_QDIM, PEER_QDIM)
        s1 = jnp.dot(keys_ref[2 * h], q_scr[pl.ds(off, D_HALF), :], preferred_element_type=F32)
        s2 = jnp.dot(keys_ref[2 * h + 1], q_scr[pl.ds(off + D_HALF, D_HALF), :],
                     preferred_element_type=F32)
        rank1 = _top16(s1, v1_scr)
        rank2 = _top16(s2, v2_scr)
        v1 = v1_scr[...]
        v2 = v2_scr[...]
        slabs = [v1[0:1] + v2]
        for a in range(1, 8):
            slabs.append(v1[a:a + 1] + v2[0:8])
        slabs.append(v1[8:16] + v2[0:1])
        cand = jnp.concatenate(slabs, axis=0)
        rest = cand
        tau = None
        for r in range(PEER_TOPK):
            tau = jnp.max(rest, axis=0, keepdims=True)
            if r < PEER_TOPK - 1:
                rest = jnp.where(rest == tau, -jnp.inf, rest)
        sel = cand >= tau
        top = v1[0:1] + v2[0:1]
        z = jnp.sum(jnp.where(sel, jnp.exp(cand - top), 0.0), axis=0, keepdims=True)
        self = jnp.where(sel, 1.0, 0.0)
        counts = [jnp.sum(self[0:16], axis=0, keepdims=True)]
        for a in range(1, 8):
            counts.append(jnp.sum(self[8 + 8 * a:16 + 8 * a], axis=0, keepdims=True))
        for a in range(8, 16):
            counts.append(self[64 + a:65 + a])
        l1 = jnp.zeros(rank1.shape, F32)
        for a in range(PEER_TOPK):
            l1 = jnp.where(rank1 == float(a), counts[a], l1)
        e1 = jnp.exp(s1 - v1[0:1])
        e2 = jnp.exp(s2 - v2[0:1]) / z
        l1w_ref[h] = _dup_bf16_words(l1)
        e1w_ref[h] = _dup_bf16_words(e1)
        r2_ref[h] = rank2.astype(BF16)
        e2_ref[h] = e2.astype(BF16)
        return carry

    lax.fori_loop(0, PEER_HEADS, head, 0)


def peer_select(hT, wqT_bf, keys_bf):
    D, N = hT.shape
    assert N % SEL_TOK == 0
    H = PEER_HEADS
    blk = pl.BlockSpec((H, N_KEYS, SEL_TOK), lambda t: (0, 0, t))
    return pl.pallas_call(
        _peer_select_kernel,
        grid=(N // SEL_TOK,),
        in_specs=[
            pl.BlockSpec((D, SEL_TOK), lambda t: (0, t)),
            pl.BlockSpec((H * PEER_QDIM, D), lambda t: (0, 0)),
            pl.BlockSpec((2 * H, N_KEYS, D_HALF), lambda t: (0, 0, 0)),
        ],
        out_specs=[blk, blk, blk, blk],
        out_shape=[
            jax.ShapeDtypeStruct((H, N_KEYS, N), jnp.uint32),
            jax.ShapeDtypeStruct((H, N_KEYS, N), jnp.uint32),
            jax.ShapeDtypeStruct((H, N_KEYS, N), BF16),
            jax.ShapeDtypeStruct((H, N_KEYS, N), BF16),
        ],
        scratch_shapes=[
            pltpu.VMEM((H * PEER_QDIM, SEL_TOK), BF16),
            pltpu.VMEM((PEER_TOPK, SEL_TOK), F32),
            pltpu.VMEM((PEER_TOPK, SEL_TOK), F32),
        ],
        compiler_params=pltpu.CompilerParams(
            dimension_semantics=("parallel",),
            vmem_limit_bytes=VMEM_LIMIT_BYTES),
        name="peer_select",
    )(hT, wqT_bf, keys_bf)


PEER_TOK = 512
PEER_EC = 1024
PEER_LANES = 256


BF16_ROWS = 16

GELU_A = -2.0 * 0.7978845608028654
GELU_B = GELU_A * 0.044715


def _gelu_tanh(x):
    return x / (1.0 + jnp.exp(x * (GELU_A + GELU_B * (x * x))))


def _peer_kernel(xT_ref, u_ref, vT_ref, l1w_ref, e1w_ref, r2_ref, e2_ref, o_ref, s_scr, w_scr):
    c = pl.program_id(1)

    @pl.when(c == 0)
    def _():
        o_ref[...] = jnp.zeros_like(o_ref)

    s_scr[...] = jnp.dot(u_ref[...], xT_ref[...], preferred_element_type=F32)

    n_i = PEER_EC // N_KEYS
    n_lg = PEER_TOK // PEER_LANES
    n_sub = N_KEYS // BF16_ROWS

    def body(it, carry):
        il = it // n_lg
        i = c * n_i + il
        lanes = pl.ds(pl.multiple_of((it % n_lg) * PEER_LANES, PEER_LANES), PEER_LANES)
        row0 = pl.multiple_of(il * N_KEYS, N_KEYS)
        gs = [jnp.zeros((BF16_ROWS, PEER_LANES), BF16) for _ in range(n_sub)]
        for h in range(PEER_HEADS):
            l1 = pltpu.bitcast(jnp.broadcast_to(l1w_ref[h, pl.ds(i, 1), lanes], (8, PEER_LANES)), BF16)
            e1 = pltpu.bitcast(jnp.broadcast_to(e1w_ref[h, pl.ds(i, 1), lanes], (8, PEER_LANES)), BF16)
            for sub in range(n_sub):
                rows = pl.ds(sub * BF16_ROWS, BF16_ROWS)
                gate = e2_ref[h, rows, lanes] * e1
                gs[sub] = gs[sub] + jnp.where(r2_ref[h, rows, lanes] < l1, gate, jnp.zeros_like(gate))
        for sub in range(n_sub):
            rows = pl.ds(row0 + sub * BF16_ROWS, BF16_ROWS)
            act = _gelu_tanh(s_scr[rows, lanes]).astype(BF16)
            w_scr[rows, lanes] = gs[sub] * act
        return carry

    lax.fori_loop(0, n_i * n_lg, body, 0)
    o_ref[...] += jnp.dot(vT_ref[...], w_scr[...], preferred_element_type=F32)


def peer_dense(xT, u_bf, vT_bf, l1w, e1w, r2, e2):
    D, N = xT.shape
    assert N % PEER_TOK == 0
    H = PEER_HEADS
    sel_blk = pl.BlockSpec((H, N_KEYS, PEER_TOK), lambda t, c: (0, 0, t))
    return pl.pallas_call(
        _peer_kernel,
        grid=(N // PEER_TOK, N_EXPERTS // PEER_EC),
        in_specs=[
            pl.BlockSpec((D, PEER_TOK), lambda t, c: (0, t)),
            pl.BlockSpec((PEER_EC, D), lambda t, c: (c, 0)),
            pl.BlockSpec((D, PEER_EC), lambda t, c: (0, c)),
            sel_blk, sel_blk, sel_blk, sel_blk,
        ],
        out_specs=pl.BlockSpec((D, PEER_TOK), lambda t, c: (0, t)),
        out_shape=jax.ShapeDtypeStruct((D, N), F32),
        scratch_shapes=[
            pltpu.VMEM((PEER_EC, PEER_TOK), F32),
            pltpu.VMEM((PEER_EC, PEER_TOK), BF16),
        ],
        compiler_params=pltpu.CompilerParams(
            dimension_semantics=("parallel", "arbitrary"),
            vmem_limit_bytes=VMEM_LIMIT_BYTES),
        name="peer_dense",
    )(xT, u_bf, vT_bf, l1w, e1w, r2, e2)


def peer(h2, wqT_bf, keys_bf, u_bf, vT_bf):
    hT = h2.T.astype(BF16)
    l1w, e1w, r2, e2 = peer_select(hT, wqT_bf, keys_bf)
    return peer_dense(hT, u_bf, vT_bf, l1w, e1w, r2, e2).T


def layer_norm(x, g, b):
    mu = jnp.mean(x, axis=-1, keepdims=True)
    var = jnp.mean(jnp.square(x - mu), axis=-1, keepdims=True)
    return (x - mu) * lax.rsqrt(var + LN_EPS) * g + b


def depthwise_conv(u, w, b):
    out = lax.conv_general_dilated(
        u, w[:, None, :], window_strides=(1,),
        padding=[(CONV_W // 2, CONV_W // 2)],
        dimension_numbers=("NWC", "WIO", "NWC"),
        feature_group_count=u.shape[-1])
    return out + b


def box_mean(u, w):
    L = u.shape[1]
    lo_off = w // 2
    hi_off = w - 1 - lo_off
    cs = jnp.concatenate([jnp.zeros_like(u[:, :1]), jnp.cumsum(u, axis=1)], axis=1)
    t = jnp.arange(L)
    lo = jnp.maximum(t - lo_off, 0)
    hi = jnp.minimum(t + hi_off, L - 1)
    cnt = (hi - lo + 1).astype(F32).reshape((1, L) + (1,) * (u.ndim - 2))
    return (cs[:, hi + 1] - cs[:, lo]) / cnt


def pool_branch(u, pool_w, pool_scale, on_grid):
    b, L, _ = u.shape
    ug = u.reshape(b, L, POOL_GROUPS, POOL_CH)
    outs = []
    for gi, w in enumerate(POOL_WINDOWS):
        ui = ug[:, :, gi]
        if on_grid:
            rows = L // GRID_W
            grid = box_mean(ui.reshape(b, rows, GRID_W, POOL_CH), w)
            grid = jnp.swapaxes(box_mean(jnp.swapaxes(grid, 1, 2), w), 1, 2)
            pooled = grid.reshape(b, L, POOL_CH)
        else:
            pooled = box_mean(ui, w)
        outs.append(pooled - ui)
    mixed = jnp.stack(outs, axis=2)
    y = jnp.einsum("blgc,gco->blgo", mixed, pool_w).reshape(b, L, D_MODEL)
    return y * pool_scale


def ssd_prepare(proj_scan, conv_w, conv_b):
    b, L, _ = proj_scan.shape
    xbc = jax.nn.silu(depthwise_conv(proj_scan[..., :COL_DTF], conv_w, conv_b))
    xh = xbc[..., :D_SSD].reshape(b, L, SSD_GROUPS, SSD_HPG, SSD_HEAD_DIM)
    bm = xbc[..., COL_B:COL_C].reshape(b, L, SSD_GROUPS, SSD_STATE)
    cm = xbc[..., COL_C:COL_DTF].reshape(b, L, SSD_GROUPS, SSD_STATE)
    dt_f = proj_scan[..., COL_DTF:COL_DTB].reshape(b, L, SSD_GROUPS, SSD_HPG)
    dt_b = proj_scan[..., COL_DTB:SCAN_COLS].reshape(b, L, SSD_GROUPS, SSD_HPG)
    return xh, bm, cm, dt_f, dt_b


def ssd_scan(xh, bm, cm, dt_raw, a_log, dt_bias, h0, reverse, with_output):
    if reverse:
        xh, bm, cm, dt_raw = [jnp.flip(t, 1) for t in (xh, bm, cm, dt_raw)]
    b, L = xh.shape[:2]
    nc = L // SSD_CHUNK
    dt = jax.nn.softplus(dt_raw + dt_bias.reshape(SSD_GROUPS, SSD_HPG))
    a = -jnp.exp(a_log.reshape(SSD_GROUPS, SSD_HPG))
    chunk = lambda t: t.reshape((b, nc, SSD_CHUNK) + t.shape[2:])
    xc, bc, cc, dtc = chunk(xh), chunk(bm), chunk(cm), chunk(dt)
    a_cum = jnp.cumsum(dtc * a, axis=2)
    decay_to_end = jnp.exp(a_cum[:, :, -1:] - a_cum)
    states = jnp.einsum("bcsgn,bcsgrp->bcgrpn", bc, (decay_to_end * dtc)[..., None] * xc)
    chunk_decay = jnp.exp(a_cum[:, :, -1])

    def step(h, inp):
        s, d = inp
        return d[..., None, None] * h + s, h

    h_last, h_in = lax.scan(step, h0, (jnp.moveaxis(states, 1, 0), jnp.moveaxis(chunk_decay, 1, 0)))
    if not with_output:
        return None, h_last
    h_in = jnp.moveaxis(h_in, 0, 1)
    seg = a_cum[:, :, :, None] - a_cum[:, :, None, :]
    lower = (jnp.arange(SSD_CHUNK)[:, None] >= jnp.arange(SSD_CHUNK)[None, :])[None, None, :, :, None, None]
    decay = jnp.exp(jnp.where(lower, seg, -jnp.inf))
    cb = jnp.einsum("bclgn,bcsgn->bclsg", cc, bc)
    wgt = cb[..., None] * decay * dtc[:, :, None]
    y = jnp.einsum("bclsgr,bcsgrp->bclgrp", wgt, xc)
    y = y + jnp.einsum("bclgn,bcgrpn->bclgrp", cc, h_in) * jnp.exp(a_cum)[..., None]
    y = y.reshape(b, L, SSD_GROUPS, SSD_HPG, SSD_HEAD_DIM)
    if reverse:
        y = jnp.flip(y, 1)
    return y, h_last


def bidir_ssd(xh, bm, cm, dt_f, dt_b, a_log, dt_bias, h0_f, h0_b, with_output):
    y_f, h_f = ssd_scan(xh, bm, cm, dt_f, a_log[0], dt_bias[0], h0_f, False, with_output)
    y_b, h_b = ssd_scan(xh, bm, cm, dt_b, a_log[1], dt_bias[1], h0_b, True, with_output)
    y = y_f + y_b if with_output else None
    return y, h_f, h_b


def mixer_out(proj, y, xh, d_skip, norm_g, w_ssd_out_bf, pool_w, pool_scale, w_o_bf, on_grid):
    b, L = y.shape[:2]
    y_pool = pool_branch(proj[..., COL_POOL:COL_GP], pool_w, pool_scale, on_grid)
    z = proj[..., COL_Z:COL_POOL]
    y = y + d_skip.reshape(SSD_GROUPS, SSD_HPG)[..., None] * xh
    yg = y.reshape(b, L, D_SSD) * jax.nn.silu(z)
    yg = yg.reshape(b, L, SSD_GROUPS, D_SSD // SSD_GROUPS)
    yg = yg * lax.rsqrt(jnp.mean(yg * yg, axis=-1, keepdims=True) + LN_EPS)
    yn = yg.reshape(b * L, D_SSD) * norm_g
    y_s = matmul(yn, w_ssd_out_bf).reshape(b, L, D_MODEL)
    merged = (jax.nn.sigmoid(proj[..., COL_GP:COL_GS]) * y_pool
              + jax.nn.sigmoid(proj[..., COL_GS:W_IN_COLS]) * y_s)
    return matmul(merged.reshape(b * L, D_MODEL), w_o_bf).reshape(b, L, D_MODEL)


def kernel(x, c, ctx, c_ctx, w_mod, b_mod, w_in, conv_w, conv_b, a_log, dt_bias, d_skip,
           ssd_norm_g, w_ssd_out, pool_w, pool_scale, w_o, ln_g, ln_b, w_q, sub_keys, u_tab, v_tab):
    D = D_MODEL
    B, L, _ = x.shape
    LC = ctx.shape[1]
    silu_c = jax.nn.silu(c)
    silu_cc = jax.nn.silu(c_ctx)
    ones_b = jnp.ones((B, 1, 1), F32)
    for l in range(DEPTH):
        last = l == DEPTH - 1
        w_mod_bf = w_mod[l].astype(BF16)
        mod_in = jnp.concatenate([silu_c, silu_cc[None]], axis=0)
        mod_all = matmul(jnp.pad(mod_in, ((0, 8 - (B + 1)), (0, 0))), w_mod_bf, tn=1536) + b_mod[l]
        mod = mod_all[:B, None, :]
        sh1, sc1, g1, sh2, sc2, g2 = jnp.split(mod, 6, axis=-1)
        mods_c = jnp.split(mod_all[B], 6)

        w_in_bf = w_in[l].astype(BF16)
        w_in_pad = jnp.pad(w_in_bf, ((0, 0), (0, 5760 - W_IN_COLS)))
        u_bf = u_tab[l].astype(BF16)
        vT_bf = v_tab[l].T.astype(BF16)
        wqT_bf = w_q[l].T.astype(BF16)
        keys_bf = sub_keys[l].reshape(2 * PEER_HEADS, N_KEYS, D_HALF).astype(BF16)
        w_ssd_out_bf = w_ssd_out[l].astype(BF16)
        w_o_bf = w_o[l].astype(BF16)

        sc_c = mods_c[1][None, None, :] * ones_b
        sh_c = mods_c[0][None, None, :] * ones_b
        proj_c = mod_matmul(ctx, sc_c, sh_c, w_in_pad, tm=LC, tn=1152)[..., :W_IN_COLS]
        xh_c, b_c, c_c, dtf_c, dtb_c = ssd_prepare(proj_c[..., :SCAN_COLS], conv_w[l], conv_b[l])
        h0 = jnp.zeros((B, SSD_GROUPS, SSD_HPG, SSD_HEAD_DIM, SSD_STATE), F32)
        y_c, hf_c, hb_c = bidir_ssd(xh_c, b_c, c_c, dtf_c, dtb_c, a_log[l], dt_bias[l], h0, h0, not last)

        proj = mod_matmul(x, sc1, sh1, w_in_pad, tm=512, tn=1152)[..., :W_IN_COLS]
        xh, bm, cm, dtf, dtb = ssd_prepare(proj[..., :SCAN_COLS], conv_w[l], conv_b[l])
        y, _, _ = bidir_ssd(xh, bm, cm, dtf, dtb, a_log[l], dt_bias[l], hf_c, hb_c, True)
        mix = mixer_out(proj, y, xh, d_skip[l], ssd_norm_g[l], w_ssd_out_bf, pool_w[l], pool_scale[l],
                        w_o_bf, True)
        x = layer_norm(DEEPNORM_ALPHA * x + g1 * mix, ln_g[l, 0], ln_b[l, 0])

        if not last:
            mix_c = mixer_out(proj_c, y_c, xh_c, d_skip[l], ssd_norm_g[l], w_ssd_out_bf, pool_w[l],
                              pool_scale[l], w_o_bf, False)
            ctx = layer_norm(DEEPNORM_ALPHA * ctx + mods_c[2] * mix_c, ln_g[l, 0], ln_b[l, 0])

        h2 = (x * (1 + sc2) + sh2).reshape(B * L, D)
        if not last:
            h2c = (ctx * (1 + mods_c[4]) + mods_c[3]).reshape(B * LC, D)
            h2 = jnp.concatenate([h2, h2c], axis=0)
        pe = peer(h2, wqT_bf, keys_bf, u_bf, vT_bf)
        x = layer_norm(DEEPNORM_ALPHA * x + g2 * pe[:B * L].reshape(B, L, D), ln_g[l, 1], ln_b[l, 1])
        if not last:
            ctx = layer_norm(DEEPNORM_ALPHA * ctx + mods_c[5] * pe[B * L:].reshape(B, LC, D),
                             ln_g[l, 1], ln_b[l, 1])
    return x
```

```python
import functools

import jax
import jax.numpy as jnp
from jax import lax
from jax.experimental import pallas as pl
from jax.experimental.pallas import tpu as pltpu

D_MODEL = 1024
DEPTH = 4
GRID_W = 64
DEEPNORM_ALPHA = (2.0 * DEPTH) ** 0.25
LN_EPS = 1e-6

SSD_HEAD_DIM = 64
D_SSD = D_MODEL
SSD_HEADS = D_SSD // SSD_HEAD_DIM
SSD_GROUPS = 4
SSD_HPG = SSD_HEADS // SSD_GROUPS
SSD_STATE = 128
SSD_CHUNK = 128
CONV_W = 5

POOL_GROUPS = 4
POOL_CH = D_MODEL // 8
POOL_WINDOWS = (2, 4, 8, 16)

PEER_HEADS = 8
N_KEYS = 128
N_EXPERTS = N_KEYS * N_KEYS
PEER_TOPK = 16
PEER_QDIM = 256
D_HALF = PEER_QDIM // 2

COL_B = D_SSD
COL_C = COL_B + SSD_GROUPS * SSD_STATE
COL_DTF = COL_C + SSD_GROUPS * SSD_STATE
COL_DTB = COL_DTF + SSD_HEADS
SCAN_COLS = COL_DTB + SSD_HEADS
COL_Z = SCAN_COLS
COL_POOL = COL_Z + D_SSD
COL_GP = COL_POOL + POOL_GROUPS * POOL_CH
COL_GS = COL_GP + D_MODEL
W_IN_COLS = COL_GS + D_MODEL

F32 = jnp.float32
BF16 = jnp.bfloat16

LANES = 128
VMEM_LIMIT_BYTES = 56 * 1024 * 1024


def _mod_matmul_kernel(a_ref, sc_ref, sh_ref, w_ref, o_ref):
    a = a_ref[0] * (1.0 + sc_ref[0]) + sh_ref[0]
    o_ref[0] = jnp.dot(a.astype(BF16), w_ref[...], preferred_element_type=F32)


def mod_matmul(a, scale, shift, w, tm=512, tn=None):
    b, L, K = a.shape
    N = w.shape[1]
    if tn is None:
        tn = N
    assert L % tm == 0 and N % tn == 0
    return pl.pallas_call(
        _mod_matmul_kernel,
        grid=(b, L // tm, N // tn),
        in_specs=[
            pl.BlockSpec((1, tm, K), lambda i, j, k: (i, j, 0)),
            pl.BlockSpec((1, 1, K), lambda i, j, k: (i, 0, 0)),
            pl.BlockSpec((1, 1, K), lambda i, j, k: (i, 0, 0)),
            pl.BlockSpec((K, tn), lambda i, j, k: (0, k)),
        ],
        out_specs=pl.BlockSpec((1, tm, tn), lambda i, j, k: (i, j, k)),
        out_shape=jax.ShapeDtypeStruct((b, L, N), F32),
        compiler_params=pltpu.CompilerParams(
            dimension_semantics=("parallel", "parallel", "arbitrary"),
            vmem_limit_bytes=VMEM_LIMIT_BYTES),
        name="mod_matmul",
    )(a, scale, shift, w)


def _matmul_kernel(a_ref, w_ref, o_ref):
    o_ref[...] = jnp.dot(a_ref[...].astype(BF16), w_ref[...], preferred_element_type=F32)


def matmul(a, w, tm=512, tn=None):
    M, K = a.shape
    N = w.shape[1]
    if tn is None:
        tn = N
    tm = min(tm, M)
    assert M % tm == 0 and N % tn == 0
    return pl.pallas_call(
        _matmul_kernel,
        grid=(M // tm, N // tn),
        in_specs=[
            pl.BlockSpec((tm, K), lambda i, k: (i, 0)),
            pl.BlockSpec((K, tn), lambda i, k: (0, k)),
        ],
        out_specs=pl.BlockSpec((tm, tn), lambda i, k: (i, k)),
        out_shape=jax.ShapeDtypeStruct((M, N), F32),
        compiler_params=pltpu.CompilerParams(
            dimension_semantics=("parallel", "arbitrary"),
            vmem_limit_bytes=VMEM_LIMIT_BYTES),
        name="matmul",
    )(a, w)


SEL_TOK = 256
NOT_RANKED = 127.0


def _dup_bf16_words(x):
    hi = lax.bitcast_convert_type(x.astype(BF16).astype(F32), jnp.uint32)
    return hi | (hi >> 16)


def _top16(s, vals_ref):
    rank = jnp.full(s.shape, NOT_RANKED, F32)
    for r in range(PEER_TOPK):
        m = jnp.max(s, axis=0, keepdims=True)
        eq = s == m
        rank = jnp.where(eq, float(r), rank)
        s = jnp.where(eq, -jnp.inf, s)
        vals_ref[r:r + 1, :] = m
    return rank


def _peer_select_kernel(hT_ref, wqT_ref, keys_ref, l1w_ref, e1w_ref, r2_ref, e2_ref,
                        q_scr, v1_scr, v2_scr):
    q_scr[...] = jnp.dot(wqT_ref[...], hT_ref[...], preferred_element_type=F32).astype(BF16)

    def head(h, carry):
        off = pl.multiple_of(h * PEER_QDIM, PEER_QDIM)
        s1 = jnp.dot(keys_ref[2 * h], q_scr[pl.ds(off, D_HALF), :], preferred_element_type=F32)
        s2 = jnp.dot(keys_ref[2 * h + 1], q_scr[pl.ds(off + D_HALF, D_HALF), :],
                     preferred_element_type=F32)
        rank1 = _top16(s1, v1_scr)
        rank2 = _top16(s2, v2_scr)
        v1 = v1_scr[...]
        v2 = v2_scr[...]
        slabs = [v1[0:1] + v2]
        for a in range(1, 8):
            slabs.append(v1[a:a + 1] + v2[0:8])
        slabs.append(v1[8:16] + v2[0:1])
        cand = jnp.concatenate(slabs, axis=0)
        rest = cand
        tau = None
        for r in range(PEER_TOPK):
            tau = jnp.max(rest, axis=0, keepdims=True)
            if r < PEER_TOPK - 1:
                rest = jnp.where(rest == tau, -jnp.inf, rest)
        sel = cand >= tau
        top = v1[0:1] + v2[0:1]
        z = jnp.sum(jnp.where(sel, jnp.exp(cand - top), 0.0), axis=0, keepdims=True)
        self = jnp.where(sel, 1.0, 0.0)
        counts = [jnp.sum(self[0:16], axis=0, keepdims=True)]
        for a in range(1, 8):
            counts.append(jnp.sum(self[8 + 8 * a:16 + 8 * a], axis=0, keepdims=True))
        for a in range(8, 16):
            counts.append(self[64 + a:65 + a])
        l1 = jnp.zeros(rank1.shape, F32)
        for a in range(PEER_TOPK):
            l1 = jnp.where(rank1 == float(a), counts[a], l1)
        e1 = jnp.exp(s1 - v1[0:1])
        e2 = jnp.exp(s2 - v2[0:1]) / z
        l1w_ref[h] = _dup_bf16_words(l1)
        e1w_ref[h] = _dup_bf16_words(e1)
        r2_ref[h] = rank2.astype(BF16)
        e2_ref[h] = e2.astype(BF16)
        return carry

    lax.fori_loop(0, PEER_HEADS, head, 0)


def peer_select(hT, wqT_bf, keys_bf):
    D, N = hT.shape
    assert N % SEL_TOK == 0
    H = PEER_HEADS
    blk = pl.BlockSpec((H, N_KEYS, SEL_TOK), lambda t: (0, 0, t))
    return pl.pallas_call(
        _peer_select_kernel,
        grid=(N // SEL_TOK,),
        in_specs=[
            pl.BlockSpec((D, SEL_TOK), lambda t: (0, t)),
            pl.BlockSpec((H * PEER_QDIM, D), lambda t: (0, 0)),
            pl.BlockSpec((2 * H, N_KEYS, D_HALF), lambda t: (0, 0, 0)),
        ],
        out_specs=[blk, blk, blk, blk],
        out_shape=[
            jax.ShapeDtypeStruct((H, N_KEYS, N), jnp.uint32),
            jax.ShapeDtypeStruct((H, N_KEYS, N), jnp.uint32),
            jax.ShapeDtypeStruct((H, N_KEYS, N), BF16),
            jax.ShapeDtypeStruct((H, N_KEYS, N), BF16),
        ],
        scratch_shapes=[
            pltpu.VMEM((H * PEER_QDIM, SEL_TOK), BF16),
            pltpu.VMEM((PEER_TOPK, SEL_TOK), F32),
            pltpu.VMEM((PEER_TOPK, SEL_TOK), F32),
        ],
        compiler_params=pltpu.CompilerParams(
            dimension_semantics=("parallel",),
            vmem_limit_bytes=VMEM_LIMIT_BYTES),
        name="peer_select",
    )(hT, wqT_bf, keys_bf)


PEER_TOK = 512
PEER_EC = 1024
PEER_LANES = 256
BF16_ROWS = 16

GELU_A = -2.0 * 0.7978845608028654
GELU_B = GELU_A * 0.044715


def _gelu_tanh(x):
    return x / (1.0 + jnp.exp(x * (GELU_A + GELU_B * (x * x))))


def _peer_kernel(xT_ref, u_ref, vT_ref, l1w_ref, e1w_ref, r2_ref, e2_ref, o_ref, s_scr, w_scr):
    c = pl.program_id(1)

    @pl.when(c == 0)
    def _():
        o_ref[...] = jnp.zeros_like(o_ref)

    s_scr[...] = jnp.dot(u_ref[...], xT_ref[...], preferred_element_type=F32)

    n_i = PEER_EC // N_KEYS
    n_lg = PEER_TOK // PEER_LANES
    n_sub = N_KEYS // BF16_ROWS

    def body(it, carry):
        il = it // n_lg
        i = c * n_i + il
        lanes = pl.ds(pl.multiple_of((it % n_lg) * PEER_LANES, PEER_LANES), PEER_LANES)
        row0 = pl.multiple_of(il * N_KEYS, N_KEYS)
        gs = [jnp.zeros((BF16_ROWS, PEER_LANES), BF16) for _ in range(n_sub)]
        for h in range(PEER_HEADS):
            l1 = pltpu.bitcast(jnp.broadcast_to(l1w_ref[h, pl.ds(i, 1), lanes], (8, PEER_LANES)), BF16)
            e1 = pltpu.bitcast(jnp.broadcast_to(e1w_ref[h, pl.ds(i, 1), lanes], (8, PEER_LANES)), BF16)
            for sub in range(n_sub):
                rows = pl.ds(sub * BF16_ROWS, BF16_ROWS)
                gate = e2_ref[h, rows, lanes] * e1
                gs[sub] = gs[sub] + jnp.where(r2_ref[h, rows, lanes] < l1, gate, jnp.zeros_like(gate))
        for sub in range(n_sub):
            rows = pl.ds(row0 + sub * BF16_ROWS, BF16_ROWS)
            act = _gelu_tanh(s_scr[rows, lanes]).astype(BF16)
            w_scr[rows, lanes] = gs[sub] * act
        return carry

    lax.fori_loop(0, n_i * n_lg, body, 0)
    o_ref[...] += jnp.dot(vT_ref[...], w_scr[...], preferred_element_type=F32)


def peer_dense(xT, u_bf, vT_bf, l1w, e1w, r2, e2):
    D, N = xT.shape
    assert N % PEER_TOK == 0
    H = PEER_HEADS
    sel_blk = pl.BlockSpec((H, N_KEYS, PEER_TOK), lambda t, c: (0, 0, t))
    return pl.pallas_call(
        _peer_kernel,
        grid=(N // PEER_TOK, N_EXPERTS // PEER_EC),
        in_specs=[
            pl.BlockSpec((D, PEER_TOK), lambda t, c: (0, t)),
            pl.BlockSpec((PEER_EC, D), lambda t, c: (c, 0)),
            pl.BlockSpec((D, PEER_EC), lambda t, c: (0, c)),
            sel_blk, sel_blk, sel_blk, sel_blk,
        ],
        out_specs=pl.BlockSpec((D, PEER_TOK), lambda t, c: (0, t)),
        out_shape=jax.ShapeDtypeStruct((D, N), F32),
        scratch_shapes=[
            pltpu.VMEM((PEER_EC, PEER_TOK), F32),
            pltpu.VMEM((PEER_EC, PEER_TOK), BF16),
        ],
        compiler_params=pltpu.CompilerParams(
            dimension_semantics=("parallel", "arbitrary"),
            vmem_limit_bytes=VMEM_LIMIT_BYTES),
        name="peer_dense",
    )(xT, u_bf, vT_bf, l1w, e1w, r2, e2)


P_X = 0
P_Z = 1024
P_GP = 2048
P_GS = 3072
P_B = 4096
P_C = 4608
P_POOL = 5120
P_DT = 5632
P_COLS = 5760


def pack_w_in(w):
    pad = jnp.zeros((w.shape[0], LANES - 2 * SSD_HEADS), w.dtype)
    return jnp.concatenate([
        w[:, :COL_B], w[:, COL_Z:COL_POOL], w[:, COL_GP:COL_GS], w[:, COL_GS:],
        w[:, COL_B:COL_C], w[:, COL_C:COL_DTF], w[:, COL_POOL:COL_GP],
        w[:, COL_DTF:SCAN_COLS], pad], axis=1).astype(BF16)


def _sigmoid(x):
    return 1.0 / (1.0 + jnp.exp(-x))


CONV_TL = 256
CONV_TC = 512
HALO = 8


def _conv_kernel(cur_ref, prev_ref, next_ref, w_ref, b_ref, o_ref):
    j = pl.program_id(1)
    nj = pl.num_programs(1)
    prev = jnp.where(j == 0, 0.0, prev_ref[0, 0])
    nxt = jnp.where(j == nj - 1, 0.0, next_ref[0, 0])
    ext = jnp.concatenate([prev, cur_ref[0], nxt], axis=0)
    tl = cur_ref.shape[1]
    acc = jnp.zeros((tl, cur_ref.shape[2]), F32) + b_ref[...]
    for k in range(CONV_W):
        off = HALO + k - CONV_W // 2
        acc = acc + w_ref[k:k + 1, :] * ext[off:off + tl]
    o_ref[0] = acc * _sigmoid(acc)


def conv_silu(proj, conv_w8, conv_b):
    b, L, _ = proj.shape
    tl = min(CONV_TL, L)
    n_c = 2048 // CONV_TC
    proj4 = proj.reshape(b, L // HALO, HALO, P_COLS)
    nh = tl // HALO
    last_h = L // HALO - 1
    colmap = lambda c: jnp.where(c < 2, c, c + (P_B // CONV_TC - 2))
    return pl.pallas_call(
        _conv_kernel,
        grid=(b, L // tl, n_c),
        in_specs=[
            pl.BlockSpec((1, tl, CONV_TC), lambda i, j, c: (i, j, colmap(c))),
            pl.BlockSpec((1, 1, HALO, CONV_TC), lambda i, j, c: (i, jnp.maximum(j * nh - 1, 0), 0, colmap(c))),
            pl.BlockSpec((1, 1, HALO, CONV_TC), lambda i, j, c: (i, jnp.minimum((j + 1) * nh, last_h), 0, colmap(c))),
            pl.BlockSpec((8, CONV_TC), lambda i, j, c: (0, c)),
            pl.BlockSpec((1, CONV_TC), lambda i, j, c: (0, c)),
        ],
        out_specs=pl.BlockSpec((1, tl, CONV_TC), lambda i, j, c: (i, j, c)),
        out_shape=jax.ShapeDtypeStruct((b, L, 2048), F32),
        compiler_params=pltpu.CompilerParams(
            dimension_semantics=("parallel", "parallel", "parallel"),
            vmem_limit_bytes=VMEM_LIMIT_BYTES),
        name="conv_silu",
    )(proj, proj4, proj4, conv_w8, conv_b)


Q = SSD_CHUNK
XB_B = 1024
XB_C = 1536


def _split3(v):
    hi = v.astype(BF16)
    r1 = v - hi.astype(F32)
    mid = r1.astype(BF16)
    lo = (r1 - mid.astype(F32)).astype(BF16)
    return hi, mid, lo


def _ssd_chunk(x_ref, dt_raw, a_lanes, bias_lanes, h_scr, y_ref, rev):
    lane0 = SSD_HEADS if rev else 0
    z = dt_raw + bias_lanes
    dt = jnp.maximum(z, 0.0) + jnp.log(1.0 + jnp.exp(-jnp.abs(z)))
    v = dt * (-jnp.exp(a_lanes))
    li = lax.broadcasted_iota(jnp.int32, (Q, Q), 0)
    si = lax.broadcasted_iota(jnp.int32, (Q, Q), 1)
    tri = (si >= li) if rev else (si <= li)
    tri_bf = jnp.where(tri, 1.0, 0.0).astype(BF16)
    hi, mid, lo = _split3(v)
    acum = (jnp.dot(tri_bf, hi, preferred_element_type=F32)
            + jnp.dot(tri_bf, mid, preferred_element_type=F32)
            + jnp.dot(tri_bf, lo, preferred_element_type=F32))
    last = 0 if rev else Q - 1
    tot = acum[last:last + 1, :]
    e_in = jnp.exp(acum)
    w_end = jnp.exp(tot - acum) * dt
    e_tot = jnp.exp(tot)
    acum_t = acum.T
    dt_t = dt.T
    for g in range(SSD_GROUPS):
        bg = x_ref[0, :, XB_B + g * SSD_STATE:XB_B + (g + 1) * SSD_STATE]
        cg = x_ref[0, :, XB_C + g * SSD_STATE:XB_C + (g + 1) * SSD_STATE].astype(BF16)
        bg_t = bg.T.astype(BF16)
        cb = jnp.dot(cg, bg_t, preferred_element_type=F32)
        for r in range(SSD_HPG):
            hd = g * SSD_HPG + r
            ln = lane0 + hd
            xh = x_ref[0, :, hd * SSD_HEAD_DIM:(hd + 1) * SSD_HEAD_DIM]
            seg = acum[:, ln:ln + 1] - acum_t[ln:ln + 1, :]
            decay = jnp.exp(jnp.where(tri, seg, -jnp.inf))
            wgt = (cb * decay * dt_t[ln:ln + 1, :]).astype(BF16)
            y = jnp.dot(wgt, xh.astype(BF16), preferred_element_type=F32)
            h_t = h_scr[hd]
            y = y + e_in[:, ln:ln + 1] * jnp.dot(cg, h_t.astype(BF16), preferred_element_type=F32)
            y_ref[0, :, hd * SSD_HEAD_DIM:(hd + 1) * SSD_HEAD_DIM] = y
            xs = (xh * w_end[:, ln:ln + 1]).astype(BF16)
            h_scr[hd] = e_tot[:, ln:ln + 1] * h_t + jnp.dot(bg_t, xs, preferred_element_type=F32)


def _ssd_kernel(xf_ref, dtf_ref, xb_ref, dtb_ref, a_ref, bias_ref, h0f_ref, h0b_ref,
                yf_ref, yb_ref, hf_ref, hb_ref, hf_scr, hb_scr):
    c = pl.program_id(1)

    @pl.when(c == 0)
    def _():
        hf_scr[...] = h0f_ref[0]
        hb_scr[...] = h0b_ref[0]

    _ssd_chunk(xf_ref, dtf_ref[0], a_ref[...], bias_ref[...], hf_scr, yf_ref, False)
    _ssd_chunk(xb_ref, dtb_ref[0], a_ref[...], bias_ref[...], hb_scr, yb_ref, True)

    @pl.when(c == pl.num_programs(1) - 1)
    def _():
        hf_ref[0] = hf_scr[...]
        hb_ref[0] = hb_scr[...]


def ssd_scan(xbc, proj, a_lanes, bias_lanes, h0f, h0b):
    b, L, _ = xbc.shape
    nc = L // Q
    st_shape = (b, SSD_HEADS, SSD_STATE, SSD_HEAD_DIM)
    st_spec = pl.BlockSpec((1, SSD_HEADS, SSD_STATE, SSD_HEAD_DIM), lambda i, c: (i, 0, 0, 0))
    dt_blk = P_DT // LANES
    return pl.pallas_call(
        _ssd_kernel,
        grid=(b, nc),
        in_specs=[
            pl.BlockSpec((1, Q, 2048), lambda i, c: (i, c, 0)),
            pl.BlockSpec((1, Q, LANES), lambda i, c: (i, c, dt_blk)),
            pl.BlockSpec((1, Q, 2048), lambda i, c: (i, nc - 1 - c, 0)),
            pl.BlockSpec((1, Q, LANES), lambda i, c: (i, nc - 1 - c, dt_blk)),
            pl.BlockSpec((1, LANES), lambda i, c: (0, 0)),
            pl.BlockSpec((1, LANES), lambda i, c: (0, 0)),
            st_spec, st_spec,
        ],
        out_specs=[
            pl.BlockSpec((1, Q, D_SSD), lambda i, c: (i, c, 0)),
            pl.BlockSpec((1, Q, D_SSD), lambda i, c: (i, nc - 1 - c, 0)),
            st_spec, st_spec,
        ],
        out_shape=[
            jax.ShapeDtypeStruct((b, L, D_SSD), F32),
            jax.ShapeDtypeStruct((b, L, D_SSD), F32),
            jax.ShapeDtypeStruct(st_shape, F32),
            jax.ShapeDtypeStruct(st_shape, F32),
        ],
        scratch_shapes=[
            pltpu.VMEM((SSD_HEADS, SSD_STATE, SSD_HEAD_DIM), F32),
            pltpu.VMEM((SSD_HEADS, SSD_STATE, SSD_HEAD_DIM), F32),
        ],
        compiler_params=pltpu.CompilerParams(
            dimension_semantics=("parallel", "arbitrary"),
            vmem_limit_bytes=VMEM_LIMIT_BYTES),
        name="ssd_scan",
    )(xbc, proj, xbc, proj, a_lanes, bias_lanes, h0f, h0b)


POOL_TILE = 512


def _box_offsets(w):
    return w // 2, w - 1 - w // 2


def _pool_group(u_ref, o_ref, pad_scr, r_scr, w, L, row_len, two_d):
    lo, hi = _box_offsets(w)
    tile = min(POOL_TILE, L)
    n_tiles = L // tile
    half = (pad_scr.shape[0] - L) // 2

    if two_d:
        n_rows = L // row_len
        pad_scr[0:half, :] = jnp.zeros((half, LANES), F32)
        pad_scr[half + L:, :] = jnp.zeros((half, LANES), F32)
        pad_scr[half:half + L, :] = u_ref[0]

        def rows_body(i, carry):
            base = pl.multiple_of(i * tile, tile)
            acc = jnp.zeros((tile, LANES), F32)
            for k in range(-lo, hi + 1):
                acc = acc + pad_scr[pl.ds(base + (half + k * row_len), tile), :]
            row = (base + lax.broadcasted_iota(jnp.int32, (tile, LANES), 0)) // row_len
            cnt = jnp.minimum(row + hi, n_rows - 1) - jnp.maximum(row - lo, 0) + 1
            r_scr[pl.ds(base, tile), :] = acc / cnt.astype(F32)
            return carry

        lax.fori_loop(0, n_tiles, rows_body, 0)

    def cols_body(i, carry):
        base = pl.multiple_of(i * tile, tile)
        u = u_ref[0, pl.ds(base, tile), :]
        t = r_scr[pl.ds(base, tile), :] if two_d else u
        col = lax.broadcasted_iota(jnp.int32, (tile, LANES), 0) % row_len
        acc = t
        for k in range(-lo, hi + 1):
            if k == 0:
                continue
            sh = pltpu.roll(t, (-k) % tile, axis=0)
            ok = (col + k >= 0) & (col + k < row_len)
            acc = acc + jnp.where(ok, sh, 0.0)
        cnt = jnp.minimum(col + hi, row_len - 1) - jnp.maximum(col - lo, 0) + 1
        o_ref[0, pl.ds(base, tile), :] = acc / cnt.astype(F32) - u
        return carry

    lax.fori_loop(0, n_tiles, cols_body, 0)


def _pool_kernel(u_ref, o_ref, pad_scr, r_scr, *, L, row_len, two_d):
    g = pl.program_id(1)
    for gi, w in enumerate(POOL_WINDOWS):
        @pl.when(g == gi)
        def _(w=w):
            _pool_group(u_ref, o_ref, pad_scr, r_scr, w, L, row_len, two_d)


def pool_mixed(proj, two_d):
    b, L, _ = proj.shape
    row_len = GRID_W if two_d else L
    pad_rows = 2 * (max(POOL_WINDOWS) // 2) * row_len if two_d else 2 * HALO
    blk0 = P_POOL // LANES
    return pl.pallas_call(
        functools.partial(_pool_kernel, L=L, row_len=row_len, two_d=two_d),
        grid=(b, POOL_GROUPS),
        in_specs=[pl.BlockSpec((1, L, LANES), lambda i, g: (i, 0, blk0 + g))],
        out_specs=pl.BlockSpec((1, L, LANES), lambda i, g: (i, 0, g)),
        out_shape=jax.ShapeDtypeStruct((b, L, POOL_GROUPS * POOL_CH), F32),
        scratch_shapes=[
            pltpu.VMEM((L + pad_rows, LANES), F32),
            pltpu.VMEM((L, LANES), F32),
        ],
        compiler_params=pltpu.CompilerParams(
            dimension_semantics=("parallel", "parallel"),
            vmem_limit_bytes=VMEM_LIMIT_BYTES),
        name="pool_mixed",
    )(proj)


MIX_TM = 256
RMS_GROUP = D_SSD // SSD_GROUPS


def _layer_norm_rows(v, g, b):
    mu = jnp.mean(v, axis=-1, keepdims=True)
    d = v - mu
    var = jnp.mean(d * d, axis=-1, keepdims=True)
    return d * lax.rsqrt(var + LN_EPS) * g + b


def _mixer_kernel(yf_ref, yb_ref, xh_ref, z_ref, gp_ref, gs_ref, mixed_ref, x_ref,
                  dskip_ref, normg_ref, wssd_ref, poolw_ref, pscale_ref, wo_ref,
                  g1_ref, lng_ref, lnb_ref, sc2_ref, sh2_ref, xo_ref, hT_ref):
    y = yf_ref[0] + yb_ref[0] + dskip_ref[...] * xh_ref[0]
    z = z_ref[0]
    yg = y * (z * _sigmoid(z))
    parts = []
    for g in range(SSD_GROUPS):
        blk = yg[:, g * RMS_GROUP:(g + 1) * RMS_GROUP]
        ms = jnp.mean(blk * blk, axis=-1, keepdims=True)
        parts.append(blk * lax.rsqrt(ms + LN_EPS))
    yn = jnp.concatenate(parts, axis=-1) * normg_ref[...]
    y_s = jnp.dot(yn.astype(BF16), wssd_ref[...], preferred_element_type=F32)
    pooled = []
    for g in range(POOL_GROUPS):
        m = mixed_ref[0][:, g * POOL_CH:(g + 1) * POOL_CH].astype(BF16)
        pooled.append(jnp.dot(m, poolw_ref[g], preferred_element_type=F32))
    y_pool = jnp.concatenate(pooled, axis=-1) * pscale_ref[...]
    merged = _sigmoid(gp_ref[0]) * y_pool + _sigmoid(gs_ref[0]) * y_s
    mix = jnp.dot(merged.astype(BF16), wo_ref[...], preferred_element_type=F32)
    xn = _layer_norm_rows(DEEPNORM_ALPHA * x_ref[0] + g1_ref[0] * mix, lng_ref[...], lnb_ref[...])
    xo_ref[0] = xn
    h2 = xn * (1.0 + sc2_ref[0]) + sh2_ref[0]
    hT_ref[...] = h2.T.astype(BF16)


def mixer_epilogue(yf, yb, xbc, proj, mixed, x, dskip, normg, wssd_bf, poolw_bf, pscale, wo_bf,
                   g1, lng, lnb, sc2, sh2):
    b, L, D = x.shape
    tm = min(MIX_TM, L)
    nj = L // tm
    tok = lambda blk: pl.BlockSpec((1, tm, D), lambda i, j, blk=blk: (i, j, blk))
    vec = pl.BlockSpec((1, D), lambda i, j: (0, 0))
    bvec = pl.BlockSpec((1, 1, D), lambda i, j: (i, 0, 0))
    full2 = pl.BlockSpec((D, D), lambda i, j: (0, 0))
    return pl.pallas_call(
        _mixer_kernel,
        grid=(b, nj),
        in_specs=[
            tok(0), tok(0), tok(0),
            tok(P_Z // D), tok(P_GP // D), tok(P_GS // D),
            pl.BlockSpec((1, tm, POOL_GROUPS * POOL_CH), lambda i, j: (i, j, 0)),
            tok(0),
            vec, vec, full2,
            pl.BlockSpec((POOL_GROUPS, POOL_CH, D // POOL_GROUPS), lambda i, j: (0, 0, 0)),
            vec, full2,
            bvec, vec, vec, bvec, bvec,
        ],
        out_specs=[
            pl.BlockSpec((1, tm, D), lambda i, j: (i, j, 0)),
            pl.BlockSpec((D, tm), lambda i, j: (0, i * nj + j)),
        ],
        out_shape=[
            jax.ShapeDtypeStruct((b, L, D), F32),
            jax.ShapeDtypeStruct((D, b * L), BF16),
        ],
        compiler_params=pltpu.CompilerParams(
            dimension_semantics=("parallel", "parallel"),
            vmem_limit_bytes=VMEM_LIMIT_BYTES),
        name="mixer_epilogue",
    )(yf, yb, xbc, proj, proj, proj, mixed, x, dskip, normg, wssd_bf, poolw_bf, pscale, wo_bf,
      g1, lng, lnb, sc2, sh2)


def _peer_out_kernel(x_ref, peT_ref, g2_ref, lng_ref, lnb_ref, o_ref):
    v = DEEPNORM_ALPHA * x_ref[0] + g2_ref[0] * peT_ref[...].T
    o_ref[0] = _layer_norm_rows(v, lng_ref[...], lnb_ref[...])


def peer_epilogue(x, peT, g2, lng, lnb):
    b, L, D = x.shape
    tm = min(MIX_TM, L)
    nj = L // tm
    vec = pl.BlockSpec((1, D), lambda i, j: (0, 0))
    return pl.pallas_call(
        _peer_out_kernel,
        grid=(b, nj),
        in_specs=[
            pl.BlockSpec((1, tm, D), lambda i, j: (i, j, 0)),
            pl.BlockSpec((D, tm), lambda i, j: (0, i * nj + j)),
            pl.BlockSpec((1, 1, D), lambda i, j: (i, 0, 0)),
            vec, vec,
        ],
        out_specs=pl.BlockSpec((1, tm, D), lambda i, j: (i, j, 0)),
        out_shape=jax.ShapeDtypeStruct((b, L, D), F32),
        compiler_params=pltpu.CompilerParams(
            dimension_semantics=("parallel", "parallel"),
            vmem_limit_bytes=VMEM_LIMIT_BYTES),
        name="peer_epilogue",
    )(x, peT, g2, lng, lnb)


def _mixing_sublayer(xin, sc1, sh1, g1, sc2, sh2, h0f, h0b, two_d, w):
    proj = mod_matmul(xin, sc1, sh1, w["w_in"], tm=min(512, xin.shape[1]), tn=1152)
    xbc = conv_silu(proj, w["conv_w8"], w["conv_b"])
    yf, yb, hf, hb = ssd_scan(xbc, proj, w["a_lanes"], w["bias_lanes"], h0f, h0b)
    mixed = pool_mixed(proj, two_d)
    xo, hT = mixer_epilogue(yf, yb, xbc, proj, mixed, xin, w["dskip"], w["normg"], w["w_ssd_out"],
                            w["pool_w"], w["pscale"], w["w_o"], g1, w["ln_g0"], w["ln_b0"], sc2, sh2)
    return xo, hT, hf, hb


def _peer_sublayer(xin, hT, g2, w):
    l1w, e1w, r2, e2 = peer_select(hT, w["wqT"], w["keys"])
    peT = peer_dense(hT, w["u"], w["vT"], l1w, e1w, r2, e2)
    return peer_epilogue(xin, peT, g2, w["ln_g1"], w["ln_b1"])


def kernel(x, c, ctx, c_ctx, w_mod, b_mod, w_in, conv_w, conv_b, a_log, dt_bias, d_skip,
           ssd_norm_g, w_ssd_out, pool_w, pool_scale, w_o, ln_g, ln_b, w_q, sub_keys, u_tab, v_tab):
    B = x.shape[0]
    silu_c = jax.nn.silu(c)
    silu_cc = jax.nn.silu(c_ctx)
    mod_in = jnp.pad(jnp.concatenate([silu_c, silu_cc[None]], axis=0), ((0, 8 - (B + 1)), (0, 0)))
    ones_b = jnp.ones((B, 1, 1), F32)
    pad_lanes = jnp.zeros((LANES - 2 * SSD_HEADS,), F32)
    state0 = jnp.zeros((B, SSD_HEADS, SSD_STATE, SSD_HEAD_DIM), F32)
    for l in range(DEPTH):
        last = l == DEPTH - 1
        mod_all = matmul(mod_in, w_mod[l].astype(BF16), tn=1536) + b_mod[l]
        sh1, sc1, g1, sh2, sc2, g2 = jnp.split(mod_all[:B, None, :], 6, axis=-1)
        mc = [m[None, None, :] * ones_b for m in jnp.split(mod_all[B], 6)]
        w = dict(
            w_in=pack_w_in(w_in[l]),
            conv_w8=jnp.pad(conv_w[l], ((0, 8 - CONV_W), (0, 0))),
            conv_b=conv_b[l][None],
            a_lanes=jnp.concatenate([a_log[l, 0], a_log[l, 1], pad_lanes])[None],
            bias_lanes=jnp.concatenate([dt_bias[l, 0], dt_bias[l, 1], pad_lanes])[None],
            dskip=jnp.repeat(d_skip[l], SSD_HEAD_DIM)[None],
            normg=ssd_norm_g[l][None],
            w_ssd_out=w_ssd_out[l].astype(BF16),
            pool_w=pool_w[l].astype(BF16),
            pscale=pool_scale[l][None],
            w_o=w_o[l].astype(BF16),
            ln_g0=ln_g[l, 0][None], ln_b0=ln_b[l, 0][None],
            ln_g1=ln_g[l, 1][None], ln_b1=ln_b[l, 1][None],
            wqT=w_q[l].T.astype(BF16),
            keys=sub_keys[l].reshape(2 * PEER_HEADS, N_KEYS, D_HALF).astype(BF16),
            u=u_tab[l].astype(BF16),
            vT=v_tab[l].T.astype(BF16),
        )
        ctx_mix, hT_c, hf_c, hb_c = _mixing_sublayer(ctx, mc[1], mc[0], mc[2], mc[4], mc[3],
                                                     state0, state0, False, w)
        x, hT, _, _ = _mixing_sublayer(x, sc1, sh1, g1, sc2, sh2, hf_c, hb_c, True, w)
        x = _peer_sublayer(x, hT, g2, w)
        if not last:
            ctx = _peer_sublayer(ctx_mix, hT_c, mc[5], w)
    return x
```

```python
import functools

import jax
import jax.numpy as jnp
from jax import lax
from jax.experimental import pallas as pl
from jax.experimental.pallas import tpu as pltpu

D_MODEL = 1024
DEPTH = 4
GRID_W = 64
DEEPNORM_ALPHA = (2.0 * DEPTH) ** 0.25
LN_EPS = 1e-6

SSD_HEAD_DIM = 64
D_SSD = D_MODEL
SSD_HEADS = D_SSD // SSD_HEAD_DIM
SSD_GROUPS = 4
SSD_HPG = SSD_HEADS // SSD_GROUPS
SSD_STATE = 128
SSD_CHUNK = 128
CONV_W = 5

POOL_GROUPS = 4
POOL_CH = D_MODEL // 8
POOL_WINDOWS = (2, 4, 8, 16)

PEER_HEADS = 8
N_KEYS = 128
N_EXPERTS = N_KEYS * N_KEYS
PEER_TOPK = 16
PEER_QDIM = 256
D_HALF = PEER_QDIM // 2

COL_B = D_SSD
COL_C = COL_B + SSD_GROUPS * SSD_STATE
COL_DTF = COL_C + SSD_GROUPS * SSD_STATE
COL_DTB = COL_DTF + SSD_HEADS
SCAN_COLS = COL_DTB + SSD_HEADS
COL_Z = SCAN_COLS
COL_POOL = COL_Z + D_SSD
COL_GP = COL_POOL + POOL_GROUPS * POOL_CH
COL_GS = COL_GP + D_MODEL
W_IN_COLS = COL_GS + D_MODEL

F32 = jnp.float32
BF16 = jnp.bfloat16

LANES = 128
VMEM_LIMIT_BYTES = 56 * 1024 * 1024


def _mod_matmul_kernel(a_ref, sc_ref, sh_ref, w_ref, o_ref):
    a = a_ref[0] * (1.0 + sc_ref[0]) + sh_ref[0]
    o_ref[0] = jnp.dot(a.astype(BF16), w_ref[...], preferred_element_type=F32)


def mod_matmul(a, scale, shift, w, tm=512, tn=None):
    b, L, K = a.shape
    N = w.shape[1]
    if tn is None:
        tn = N
    assert L % tm == 0 and N % tn == 0
    return pl.pallas_call(
        _mod_matmul_kernel,
        grid=(b, L // tm, N // tn),
        in_specs=[
            pl.BlockSpec((1, tm, K), lambda i, j, k: (i, j, 0)),
            pl.BlockSpec((1, 1, K), lambda i, j, k: (i, 0, 0)),
            pl.BlockSpec((1, 1, K), lambda i, j, k: (i, 0, 0)),
            pl.BlockSpec((K, tn), lambda i, j, k: (0, k)),
        ],
        out_specs=pl.BlockSpec((1, tm, tn), lambda i, j, k: (i, j, k)),
        out_shape=jax.ShapeDtypeStruct((b, L, N), F32),
        compiler_params=pltpu.CompilerParams(
            dimension_semantics=("parallel", "parallel", "arbitrary"),
            vmem_limit_bytes=VMEM_LIMIT_BYTES),
        name="mod_matmul",
    )(a, scale, shift, w)


def _matmul_kernel(a_ref, w_ref, o_ref):
    o_ref[...] = jnp.dot(a_ref[...].astype(BF16), w_ref[...], preferred_element_type=F32)


def matmul(a, w, tm=512, tn=None):
    M, K = a.shape
    N = w.shape[1]
    if tn is None:
        tn = N
    tm = min(tm, M)
    assert M % tm == 0 and N % tn == 0
    return pl.pallas_call(
        _matmul_kernel,
        grid=(M // tm, N // tn),
        in_specs=[
            pl.BlockSpec((tm, K), lambda i, k: (i, 0)),
            pl.BlockSpec((K, tn), lambda i, k: (0, k)),
        ],
        out_specs=pl.BlockSpec((tm, tn), lambda i, k: (i, k)),
        out_shape=jax.ShapeDtypeStruct((M, N), F32),
        compiler_params=pltpu.CompilerParams(
            dimension_semantics=("parallel", "arbitrary"),
            vmem_limit_bytes=VMEM_LIMIT_BYTES),
        name="matmul",
    )(a, w)


SEL_TOK = 256
NOT_RANKED = 127.0


def _dup_bf16_words(x):
    hi = lax.bitcast_convert_type(x.astype(BF16).astype(F32), jnp.uint32)
    return hi | (hi >> 16)


def _top16(s, vals_ref, want_rank):
    rank = jnp.full(s.shape, NOT_RANKED, F32) if want_rank else None
    for r in range(PEER_TOPK):
        m = jnp.max(s, axis=0, keepdims=True)
        eq = s == m
        if want_rank:
            rank = jnp.where(eq, float(r), rank)
        s = jnp.where(eq, -jnp.inf, s)
        vals_ref[r:r + 1, :] = m
    return rank


def _peer_select_kernel(hT_ref, wqT_ref, keys_ref, l1w_ref, e1w_ref, r2_ref, e2_ref,
                        q_scr, v1_scr, v2_scr):
    q_scr[...] = jnp.dot(wqT_ref[...], hT_ref[...], preferred_element_type=F32).astype(BF16)

    def head(h, carry):
        off = pl.multiple_of(h * PEER_QDIM, PEER_QDIM)
        s1 = jnp.dot(keys_ref[2 * h], q_scr[pl.ds(off, D_HALF), :], preferred_element_type=F32)
        s2 = jnp.dot(keys_ref[2 * h + 1], q_scr[pl.ds(off + D_HALF, D_HALF), :],
                     preferred_element_type=F32)
        _top16(s1, v1_scr, False)
        rank2 = _top16(s2, v2_scr, True)
        v1 = v1_scr[...]
        v2 = v2_scr[...]
        slabs = [v1[0:1] + v2]
        for a in range(1, 8):
            slabs.append(v1[a:a + 1] + v2[0:8])
        slabs.append(v1[8:16] + v2[0:1])
        cand = jnp.concatenate(slabs, axis=0)
        rest = cand
        tau = None
        for r in range(PEER_TOPK):
            tau = jnp.max(rest, axis=0, keepdims=True)
            if r < PEER_TOPK - 1:
                rest = jnp.where(rest == tau, -jnp.inf, rest)
        sel = cand >= tau
        top = v1[0:1] + v2[0:1]
        z = jnp.sum(jnp.where(sel, jnp.exp(cand - top), 0.0), axis=0, keepdims=True)
        self = jnp.where(sel, 1.0, 0.0)
        counts = [jnp.sum(self[0:16], axis=0, keepdims=True)]
        for a in range(1, 8):
            counts.append(jnp.sum(self[8 + 8 * a:16 + 8 * a], axis=0, keepdims=True))
        for a in range(8, 16):
            counts.append(self[64 + a:65 + a])
        l1 = jnp.zeros(s1.shape, F32)
        for a in range(PEER_TOPK):
            l1 = jnp.where(s1 == v1[a:a + 1], counts[a], l1)
        e1 = jnp.exp(s1 - v1[0:1])
        e2 = jnp.exp(s2 - v2[0:1]) / z
        l1w_ref[h] = _dup_bf16_words(l1)
        e1w_ref[h] = _dup_bf16_words(e1)
        r2_ref[h] = rank2.astype(BF16)
        e2_ref[h] = e2.astype(BF16)
        return carry

    lax.fori_loop(0, PEER_HEADS, head, 0, unroll=2)


def peer_select(hT, wqT_bf, keys_bf):
    D, N = hT.shape
    assert N % SEL_TOK == 0
    H = PEER_HEADS
    blk = pl.BlockSpec((H, N_KEYS, SEL_TOK), lambda t: (0, 0, t))
    return pl.pallas_call(
        _peer_select_kernel,
        grid=(N // SEL_TOK,),
        in_specs=[
            pl.BlockSpec((D, SEL_TOK), lambda t: (0, t)),
            pl.BlockSpec((H * PEER_QDIM, D), lambda t: (0, 0)),
            pl.BlockSpec((2 * H, N_KEYS, D_HALF), lambda t: (0, 0, 0)),
        ],
        out_specs=[blk, blk, blk, blk],
        out_shape=[
            jax.ShapeDtypeStruct((H, N_KEYS, N), jnp.uint32),
            jax.ShapeDtypeStruct((H, N_KEYS, N), jnp.uint32),
            jax.ShapeDtypeStruct((H, N_KEYS, N), BF16),
            jax.ShapeDtypeStruct((H, N_KEYS, N), BF16),
        ],
        scratch_shapes=[
            pltpu.VMEM((H * PEER_QDIM, SEL_TOK), BF16),
            pltpu.VMEM((PEER_TOPK, SEL_TOK), F32),
            pltpu.VMEM((PEER_TOPK, SEL_TOK), F32),
        ],
        compiler_params=pltpu.CompilerParams(
            dimension_semantics=("parallel",),
            vmem_limit_bytes=VMEM_LIMIT_BYTES),
        name="peer_select",
    )(hT, wqT_bf, keys_bf)


PEER_TOK = 512
PEER_EC = 1024
PEER_LANES = 256
BF16_ROWS = 16

GELU_A = -2.0 * 0.7978845608028654
GELU_B = GELU_A * 0.044715


def _gelu_tanh(x):
    return x / (1.0 + jnp.exp(x * (GELU_A + GELU_B * (x * x))))


N_CHUNKS = N_EXPERTS // PEER_EC
PIPE_DEPTH = 2


N_TILE_ROWS = PEER_EC // N_KEYS
N_LANE_GROUPS = PEER_TOK // PEER_LANES
MXU_K = 256
N_K_TILES = PEER_EC // MXU_K


def _peer_gate_rows(chunk, il, s_in, w_out, l1w_ref, e1w_ref, r2_ref, e2_ref):
    n_sub = N_KEYS // BF16_ROWS
    i = chunk * N_TILE_ROWS + il
    for lg in range(N_LANE_GROUPS):
        lanes = slice(lg * PEER_LANES, (lg + 1) * PEER_LANES)
        gs = [jnp.zeros((BF16_ROWS, PEER_LANES), BF16) for _ in range(n_sub)]
        for h in range(PEER_HEADS):
            l1 = pltpu.bitcast(jnp.broadcast_to(l1w_ref[h, pl.ds(i, 1), lanes], (8, PEER_LANES)), BF16)
            e1 = pltpu.bitcast(jnp.broadcast_to(e1w_ref[h, pl.ds(i, 1), lanes], (8, PEER_LANES)), BF16)
            for sub in range(n_sub):
                rows = slice(sub * BF16_ROWS, (sub + 1) * BF16_ROWS)
                gate = e2_ref[h, rows, lanes] * e1
                gs[sub] = gs[sub] + jnp.where(r2_ref[h, rows, lanes] < l1, gate, jnp.zeros_like(gate))
        for sub in range(n_sub):
            rows = slice(il * N_KEYS + sub * BF16_ROWS, il * N_KEYS + (sub + 1) * BF16_ROWS)
            act = _gelu_tanh(s_in[rows, lanes]).astype(BF16)
            w_out[rows, lanes] = gs[sub] * act


def _peer_kernel(xT_ref, u_ref, vT_ref, l1w_ref, e1w_ref, r2_ref, e2_ref, o_ref,
                 s_a, s_b, w_a, w_b, acc_scr, *, n_work):
    s = pl.program_id(0)

    @pl.when(s == 0)
    def _():
        s_a[...] = jnp.zeros_like(s_a)
        s_b[...] = jnp.zeros_like(s_b)
        w_a[...] = jnp.zeros_like(w_a)
        w_b[...] = jnp.zeros_like(w_b)

    drain = s - PIPE_DEPTH
    drain_chunk = drain % N_CHUNKS

    @pl.when((drain_chunk == 0) | (s == 0))
    def _():
        acc_scr[...] = jnp.zeros_like(acc_scr)

    gate_chunk = jnp.clip(s - 1, 0, n_work - 1) % N_CHUNKS

    def step(s_out, s_in, w_out, w_in):
        pieces = [(lg, k) for lg in range(N_LANE_GROUPS) for k in range(N_K_TILES)]
        assert len(pieces) == N_TILE_ROWS
        for il, (lg, k) in enumerate(pieces):
            lanes = slice(lg * PEER_LANES, (lg + 1) * PEER_LANES)
            ks = slice(k * MXU_K, (k + 1) * MXU_K)
            _peer_gate_rows(gate_chunk, il, s_in, w_out, l1w_ref, e1w_ref, r2_ref, e2_ref)
            part = jnp.dot(u_ref[:, ks], xT_ref[ks, lanes], preferred_element_type=F32)
            if k == 0:
                s_out[:, lanes] = part
            else:
                s_out[:, lanes] += part
            acc_scr[:, lanes] += jnp.dot(vT_ref[:, ks], w_in[ks, lanes], preferred_element_type=F32)

    @pl.when(s % 2 == 0)
    def _():
        step(s_a, s_b, w_b, w_a)

    @pl.when(s % 2 == 1)
    def _():
        step(s_b, s_a, w_a, w_b)

    @pl.when((drain >= 0) & (drain_chunk == N_CHUNKS - 1))
    def _():
        o_ref[...] = acc_scr[...]


def peer_dense(xT, u_bf, vT_bf, l1w, e1w, r2, e2):
    D, N = xT.shape
    assert N % PEER_TOK == 0
    H = PEER_HEADS
    n_work = (N // PEER_TOK) * N_CHUNKS
    fill = lambda s: jnp.minimum(s, n_work - 1)
    gate = lambda s: jnp.clip(s - 1, 0, n_work - 1)
    drain = lambda s: jnp.maximum(s - PIPE_DEPTH, 0)
    sel_blk = pl.BlockSpec((H, N_KEYS, PEER_TOK), lambda s: (0, 0, gate(s) // N_CHUNKS))
    return pl.pallas_call(
        functools.partial(_peer_kernel, n_work=n_work),
        grid=(n_work + PIPE_DEPTH,),
        in_specs=[
            pl.BlockSpec((D, PEER_TOK), lambda s: (0, fill(s) // N_CHUNKS)),
            pl.BlockSpec((PEER_EC, D), lambda s: (fill(s) % N_CHUNKS, 0)),
            pl.BlockSpec((D, PEER_EC), lambda s: (0, drain(s) % N_CHUNKS)),
            sel_blk, sel_blk, sel_blk, sel_blk,
        ],
        out_specs=pl.BlockSpec((D, PEER_TOK), lambda s: (0, drain(s) // N_CHUNKS)),
        out_shape=jax.ShapeDtypeStruct((D, N), F32),
        scratch_shapes=[
            pltpu.VMEM((PEER_EC, PEER_TOK), F32),
            pltpu.VMEM((PEER_EC, PEER_TOK), F32),
            pltpu.VMEM((PEER_EC, PEER_TOK), BF16),
            pltpu.VMEM((PEER_EC, PEER_TOK), BF16),
            pltpu.VMEM((D, PEER_TOK), F32),
        ],
        compiler_params=pltpu.CompilerParams(
            dimension_semantics=("arbitrary",),
            vmem_limit_bytes=VMEM_LIMIT_BYTES),
        name="peer_dense",
    )(xT, u_bf, vT_bf, l1w, e1w, r2, e2)


P_X = 0
P_Z = 1024
P_GP = 2048
P_GS = 3072
P_B = 4096
P_C = 4608
P_POOL = 5120
P_DT = 5632
P_COLS = 5760


def pack_w_in(w):
    pad = jnp.zeros((w.shape[0], LANES - 2 * SSD_HEADS), w.dtype)
    return jnp.concatenate([
        w[:, :COL_B], w[:, COL_Z:COL_POOL], w[:, COL_GP:COL_GS], w[:, COL_GS:],
        w[:, COL_B:COL_C], w[:, COL_C:COL_DTF], w[:, COL_POOL:COL_GP],
        w[:, COL_DTF:SCAN_COLS], pad], axis=1).astype(BF16)


def _sigmoid(x):
    return 1.0 / (1.0 + jnp.exp(-x))


CONV_TL = 256
CONV_TC = 512
HALO = 8


def _conv_kernel(cur_ref, prev_ref, next_ref, w_ref, b_ref, o_ref):
    j = pl.program_id(1)
    nj = pl.num_programs(1)
    prev = jnp.where(j == 0, 0.0, prev_ref[0, 0])
    nxt = jnp.where(j == nj - 1, 0.0, next_ref[0, 0])
    ext = jnp.concatenate([prev, cur_ref[0], nxt], axis=0)
    tl = cur_ref.shape[1]
    acc = jnp.zeros((tl, cur_ref.shape[2]), F32) + b_ref[...]
    for k in range(CONV_W):
        off = HALO + k - CONV_W // 2
        acc = acc + w_ref[k:k + 1, :] * ext[off:off + tl]
    o_ref[0] = acc * _sigmoid(acc)


def conv_silu(proj, conv_w8, conv_b):
    b, L, _ = proj.shape
    tl = min(CONV_TL, L)
    n_c = 2048 // CONV_TC
    proj4 = proj.reshape(b, L // HALO, HALO, P_COLS)
    nh = tl // HALO
    last_h = L // HALO - 1
    colmap = lambda c: jnp.where(c < 2, c, c + (P_B // CONV_TC - 2))
    return pl.pallas_call(
        _conv_kernel,
        grid=(b, L // tl, n_c),
        in_specs=[
            pl.BlockSpec((1, tl, CONV_TC), lambda i, j, c: (i, j, colmap(c))),
            pl.BlockSpec((1, 1, HALO, CONV_TC), lambda i, j, c: (i, jnp.maximum(j * nh - 1, 0), 0, colmap(c))),
            pl.BlockSpec((1, 1, HALO, CONV_TC), lambda i, j, c: (i, jnp.minimum((j + 1) * nh, last_h), 0, colmap(c))),
            pl.BlockSpec((8, CONV_TC), lambda i, j, c: (0, c)),
            pl.BlockSpec((1, CONV_TC), lambda i, j, c: (0, c)),
        ],
        out_specs=pl.BlockSpec((1, tl, CONV_TC), lambda i, j, c: (i, j, c)),
        out_shape=jax.ShapeDtypeStruct((b, L, 2048), F32),
        compiler_params=pltpu.CompilerParams(
            dimension_semantics=("parallel", "parallel", "parallel"),
            vmem_limit_bytes=VMEM_LIMIT_BYTES),
        name="conv_silu",
    )(proj, proj4, proj4, conv_w8, conv_b)


Q = SSD_CHUNK
XB_B = 1024
XB_C = 1536


def _split3(v):
    hi = v.astype(BF16)
    r1 = v - hi.astype(F32)
    mid = r1.astype(BF16)
    lo = (r1 - mid.astype(F32)).astype(BF16)
    return hi, mid, lo


SSD_PAIRS = SSD_HEADS // 2
PAIR_W = 2 * SSD_HEAD_DIM


def head_lane_selector():
    k = jnp.arange(2 * LANES)[:, None] % LANES
    col_head = jnp.arange(D_SSD)[None, :] // SSD_HEAD_DIM
    return jnp.stack([k == col_head, k == col_head + SSD_HEADS]).astype(BF16)


def _ssd_decays(dt_raw, a_lanes, bias_lanes, sel, rev):
    z = dt_raw + bias_lanes
    dt = jnp.maximum(z, 0.0) + jnp.log(1.0 + jnp.exp(-jnp.abs(z)))
    v = dt * (-jnp.exp(a_lanes))
    li = lax.broadcasted_iota(jnp.int32, (Q, Q), 0)
    si = lax.broadcasted_iota(jnp.int32, (Q, Q), 1)
    tri = (si >= li) if rev else (si <= li)
    tri_bf = jnp.where(tri, 1.0, 0.0).astype(BF16)
    hi, mid, lo = _split3(v)
    acum = (jnp.dot(tri_bf, hi, preferred_element_type=F32)
            + jnp.dot(tri_bf, mid, preferred_element_type=F32)
            + jnp.dot(tri_bf, lo, preferred_element_type=F32))
    last = 0 if rev else Q - 1
    tot = acum[last:last + 1, :]
    stack = jnp.concatenate([jnp.exp(acum), jnp.exp(tot - acum) * dt,
                             jnp.broadcast_to(jnp.exp(tot), (8, LANES))], axis=0)
    s_hi = stack.astype(BF16)
    s_mid = (stack - s_hi.astype(F32)).astype(BF16)
    spread = jnp.dot(jnp.concatenate([s_hi, s_mid], axis=1), sel, preferred_element_type=F32)
    return dict(tri=tri, acum=acum, acum_t=acum.T, dt_t=dt.T,
                e_in=spread[0:Q], w_end=spread[Q:2 * Q], e_tot=spread[2 * Q:2 * Q + 1],
                lane0=SSD_HEADS if rev else 0)


def _ssd_group(x_ref, g):
    bg = x_ref[0, :, XB_B + g * SSD_STATE:XB_B + (g + 1) * SSD_STATE]
    cg = x_ref[0, :, XB_C + g * SSD_STATE:XB_C + (g + 1) * SSD_STATE].astype(BF16)
    bg_t = bg.T.astype(BF16)
    cb = jnp.dot(cg, bg_t, preferred_element_type=F32)
    return cg, bg_t, cb


def _ssd_pair(d, grp, x_ref, h_scr, y_ref, pr):
    cg, bg_t, cb = grp
    cols = slice(pr * PAIR_W, (pr + 1) * PAIR_W)
    xh = x_ref[0, :, cols]
    xh_bf = xh.astype(BF16)
    halves = []
    for e in range(2):
        ln = d["lane0"] + 2 * pr + e
        seg = d["acum"][:, ln:ln + 1] - d["acum_t"][ln:ln + 1, :]
        decay = jnp.exp(jnp.where(d["tri"], seg, -jnp.inf))
        wgt = (cb * decay * d["dt_t"][ln:ln + 1, :]).astype(BF16)
        halves.append(jnp.dot(wgt, xh_bf, preferred_element_type=F32))
    first = lax.broadcasted_iota(jnp.int32, (Q, PAIR_W), 1) < SSD_HEAD_DIM
    h_t = h_scr[pr]
    y = (jnp.where(first, halves[0], halves[1])
         + d["e_in"][:, cols] * jnp.dot(cg, h_t.astype(BF16), preferred_element_type=F32))
    y_ref[0, :, cols] = y
    xs = (xh * d["w_end"][:, cols]).astype(BF16)
    h_scr[pr] = d["e_tot"][:, cols] * h_t + jnp.dot(bg_t, xs, preferred_element_type=F32)


def _ssd_kernel(xf_ref, dtf_ref, xb_ref, dtb_ref, a_ref, bias_ref, sel_ref, h0f_ref, h0b_ref,
                yf_ref, yb_ref, hf_ref, hb_ref, hf_scr, hb_scr):
    c = pl.program_id(1)

    @pl.when(c == 0)
    def _():
        hf_scr[...] = h0f_ref[0]
        hb_scr[...] = h0b_ref[0]

    df = _ssd_decays(dtf_ref[0], a_ref[...], bias_ref[...], sel_ref[0], False)
    db = _ssd_decays(dtb_ref[0], a_ref[...], bias_ref[...], sel_ref[1], True)
    pairs_per_group = SSD_HPG // 2
    for g in range(SSD_GROUPS):
        gf = _ssd_group(xf_ref, g)
        gb = _ssd_group(xb_ref, g)
        for r in range(pairs_per_group):
            pr = g * pairs_per_group + r
            _ssd_pair(df, gf, xf_ref, hf_scr, yf_ref, pr)
            _ssd_pair(db, gb, xb_ref, hb_scr, yb_ref, pr)

    @pl.when(c == pl.num_programs(1) - 1)
    def _():
        hf_ref[0] = hf_scr[...]
        hb_ref[0] = hb_scr[...]


def ssd_scan(xbc, proj, a_lanes, bias_lanes, sel, h0f, h0b):
    b, L, _ = xbc.shape
    nc = L // Q
    st_shape = (b, SSD_PAIRS, SSD_STATE, PAIR_W)
    st_spec = pl.BlockSpec((1, SSD_PAIRS, SSD_STATE, PAIR_W), lambda i, c: (i, 0, 0, 0))
    dt_blk = P_DT // LANES
    return pl.pallas_call(
        _ssd_kernel,
        grid=(b, nc),
        in_specs=[
            pl.BlockSpec((1, Q, 2048), lambda i, c: (i, c, 0)),
            pl.BlockSpec((1, Q, LANES), lambda i, c: (i, c, dt_blk)),
            pl.BlockSpec((1, Q, 2048), lambda i, c: (i, nc - 1 - c, 0)),
            pl.BlockSpec((1, Q, LANES), lambda i, c: (i, nc - 1 - c, dt_blk)),
            pl.BlockSpec((1, LANES), lambda i, c: (0, 0)),
            pl.BlockSpec((1, LANES), lambda i, c: (0, 0)),
            pl.BlockSpec((2, 2 * LANES, D_SSD), lambda i, c: (0, 0, 0)),
            st_spec, st_spec,
        ],
        out_specs=[
            pl.BlockSpec((1, Q, D_SSD), lambda i, c: (i, c, 0)),
            pl.BlockSpec((1, Q, D_SSD), lambda i, c: (i, nc - 1 - c, 0)),
            st_spec, st_spec,
        ],
        out_shape=[
            jax.ShapeDtypeStruct((b, L, D_SSD), F32),
            jax.ShapeDtypeStruct((b, L, D_SSD), F32),
            jax.ShapeDtypeStruct(st_shape, F32),
            jax.ShapeDtypeStruct(st_shape, F32),
        ],
        scratch_shapes=[
            pltpu.VMEM((SSD_PAIRS, SSD_STATE, PAIR_W), F32),
            pltpu.VMEM((SSD_PAIRS, SSD_STATE, PAIR_W), F32),
        ],
        compiler_params=pltpu.CompilerParams(
            dimension_semantics=("parallel", "arbitrary"),
            vmem_limit_bytes=VMEM_LIMIT_BYTES),
        name="ssd_scan",
    )(xbc, proj, xbc, proj, a_lanes, bias_lanes, sel, h0f, h0b)


POOL_TILE = 512


def _box_offsets(w):
    return w // 2, w - 1 - w // 2


def _pool_group(u_ref, o_ref, pad_scr, r_scr, w, L, row_len, two_d):
    lo, hi = _box_offsets(w)
    tile = min(POOL_TILE, L)
    n_tiles = L // tile
    half = (pad_scr.shape[0] - L) // 2

    if two_d:
        n_rows = L // row_len
        pad_scr[0:half, :] = jnp.zeros((half, LANES), F32)
        pad_scr[half + L:, :] = jnp.zeros((half, LANES), F32)
        pad_scr[half:half + L, :] = u_ref[0]

        def rows_body(i, carry):
            base = pl.multiple_of(i * tile, tile)
            acc = jnp.zeros((tile, LANES), F32)
            for k in range(-lo, hi + 1):
                acc = acc + pad_scr[pl.ds(base + (half + k * row_len), tile), :]
            row = (base + lax.broadcasted_iota(jnp.int32, (tile, LANES), 0)) // row_len
            cnt = jnp.minimum(row + hi, n_rows - 1) - jnp.maximum(row - lo, 0) + 1
            r_scr[pl.ds(base, tile), :] = acc / cnt.astype(F32)
            return carry

        lax.fori_loop(0, n_tiles, rows_body, 0)

    def cols_body(i, carry):
        base = pl.multiple_of(i * tile, tile)
        u = u_ref[0, pl.ds(base, tile), :]
        t = r_scr[pl.ds(base, tile), :] if two_d else u
        col = lax.broadcasted_iota(jnp.int32, (tile, LANES), 0) % row_len
        acc = t
        for k in range(-lo, hi + 1):
            if k == 0:
                continue
            sh = pltpu.roll(t, (-k) % tile, axis=0)
            ok = (col + k >= 0) & (col + k < row_len)
            acc = acc + jnp.where(ok, sh, 0.0)
        cnt = jnp.minimum(col + hi, row_len - 1) - jnp.maximum(col - lo, 0) + 1
        o_ref[0, pl.ds(base, tile), :] = acc / cnt.astype(F32) - u
        return carry

    lax.fori_loop(0, n_tiles, cols_body, 0)


def _pool_kernel(u_ref, o_ref, pad_scr, r_scr, *, L, row_len, two_d):
    g = pl.program_id(1)
    for gi, w in enumerate(POOL_WINDOWS):
        @pl.when(g == gi)
        def _(w=w):
            _pool_group(u_ref, o_ref, pad_scr, r_scr, w, L, row_len, two_d)


def pool_mixed(proj, two_d):
    b, L, _ = proj.shape
    row_len = GRID_W if two_d else L
    pad_rows = 2 * (max(POOL_WINDOWS) // 2) * row_len if two_d else 2 * HALO
    blk0 = P_POOL // LANES
    return pl.pallas_call(
        functools.partial(_pool_kernel, L=L, row_len=row_len, two_d=two_d),
        grid=(b, POOL_GROUPS),
        in_specs=[pl.BlockSpec((1, L, LANES), lambda i, g: (i, 0, blk0 + g))],
        out_specs=pl.BlockSpec((1, L, LANES), lambda i, g: (i, 0, g)),
        out_shape=jax.ShapeDtypeStruct((b, L, POOL_GROUPS * POOL_CH), F32),
        scratch_shapes=[
            pltpu.VMEM((L + pad_rows, LANES), F32),
            pltpu.VMEM((L, LANES), F32),
        ],
        compiler_params=pltpu.CompilerParams(
            dimension_semantics=("parallel", "parallel"),
            vmem_limit_bytes=VMEM_LIMIT_BYTES),
        name="pool_mixed",
    )(proj)


MIX_TM = 256
RMS_GROUP = D_SSD // SSD_GROUPS


def _layer_norm_rows(v, g, b):
    mu = jnp.mean(v, axis=-1, keepdims=True)
    d = v - mu
    var = jnp.mean(d * d, axis=-1, keepdims=True)
    return d * lax.rsqrt(var + LN_EPS) * g + b


def _mixer_kernel(yf_ref, yb_ref, xh_ref, z_ref, gp_ref, gs_ref, mixed_ref, x_ref,
                  dskip_ref, normg_ref, wssd_ref, poolw_ref, pscale_ref, wo_ref,
                  g1_ref, lng_ref, lnb_ref, sc2_ref, sh2_ref, xo_ref, hT_ref):
    y = yf_ref[0] + yb_ref[0] + dskip_ref[...] * xh_ref[0]
    z = z_ref[0]
    yg = y * (z * _sigmoid(z))
    parts = []
    for g in range(SSD_GROUPS):
        blk = yg[:, g * RMS_GROUP:(g + 1) * RMS_GROUP]
        ms = jnp.mean(blk * blk, axis=-1, keepdims=True)
        parts.append(blk * lax.rsqrt(ms + LN_EPS))
    yn = jnp.concatenate(parts, axis=-1) * normg_ref[...]
    y_s = jnp.dot(yn.astype(BF16), wssd_ref[...], preferred_element_type=F32)
    pooled = []
    for g in range(POOL_GROUPS):
        m = mixed_ref[0][:, g * POOL_CH:(g + 1) * POOL_CH].astype(BF16)
        pooled.append(jnp.dot(m, poolw_ref[g], preferred_element_type=F32))
    y_pool = jnp.concatenate(pooled, axis=-1) * pscale_ref[...]
    merged = _sigmoid(gp_ref[0]) * y_pool + _sigmoid(gs_ref[0]) * y_s
    mix = jnp.dot(merged.astype(BF16), wo_ref[...], preferred_element_type=F32)
    xn = _layer_norm_rows(DEEPNORM_ALPHA * x_ref[0] + g1_ref[0] * mix, lng_ref[...], lnb_ref[...])
    xo_ref[0] = xn
    h2 = xn * (1.0 + sc2_ref[0]) + sh2_ref[0]
    hT_ref[...] = h2.T.astype(BF16)


def mixer_epilogue(yf, yb, xbc, proj, mixed, x, dskip, normg, wssd_bf, poolw_bf, pscale, wo_bf,
                   g1, lng, lnb, sc2, sh2):
    b, L, D = x.shape
    tm = min(MIX_TM, L)
    nj = L // tm
    tok = lambda blk: pl.BlockSpec((1, tm, D), lambda i, j, blk=blk: (i, j, blk))
    vec = pl.BlockSpec((1, D), lambda i, j: (0, 0))
    bvec = pl.BlockSpec((1, 1, D), lambda i, j: (i, 0, 0))
    full2 = pl.BlockSpec((D, D), lambda i, j: (0, 0))
    return pl.pallas_call(
        _mixer_kernel,
        grid=(b, nj),
        in_specs=[
            tok(0), tok(0), tok(0),
            tok(P_Z // D), tok(P_GP // D), tok(P_GS // D),
            pl.BlockSpec((1, tm, POOL_GROUPS * POOL_CH), lambda i, j: (i, j, 0)),
            tok(0),
            vec, vec, full2,
            pl.BlockSpec((POOL_GROUPS, POOL_CH, D // POOL_GROUPS), lambda i, j: (0, 0, 0)),
            vec, full2,
            bvec, vec, vec, bvec, bvec,
        ],
        out_specs=[
            pl.BlockSpec((1, tm, D), lambda i, j: (i, j, 0)),
            pl.BlockSpec((D, tm), lambda i, j: (0, i * nj + j)),
        ],
        out_shape=[
            jax.ShapeDtypeStruct((b, L, D), F32),
            jax.ShapeDtypeStruct((D, b * L), BF16),
        ],
        compiler_params=pltpu.CompilerParams(
            dimension_semantics=("parallel", "parallel"),
            vmem_limit_bytes=VMEM_LIMIT_BYTES),
        name="mixer_epilogue",
    )(yf, yb, xbc, proj, proj, proj, mixed, x, dskip, normg, wssd_bf, poolw_bf, pscale, wo_bf,
      g1, lng, lnb, sc2, sh2)


def _peer_out_kernel(x_ref, peT_ref, g2_ref, lng_ref, lnb_ref, o_ref):
    v = DEEPNORM_ALPHA * x_ref[0] + g2_ref[0] * peT_ref[...].T
    o_ref[0] = _layer_norm_rows(v, lng_ref[...], lnb_ref[...])


def peer_epilogue(x, peT, g2, lng, lnb):
    b, L, D = x.shape
    tm = min(MIX_TM, L)
    nj = L // tm
    vec = pl.BlockSpec((1, D), lambda i, j: (0, 0))
    return pl.pallas_call(
        _peer_out_kernel,
        grid=(b, nj),
        in_specs=[
            pl.BlockSpec((1, tm, D), lambda i, j: (i, j, 0)),
            pl.BlockSpec((D, tm), lambda i, j: (0, i * nj + j)),
            pl.BlockSpec((1, 1, D), lambda i, j: (i, 0, 0)),
            vec, vec,
        ],
        out_specs=pl.BlockSpec((1, tm, D), lambda i, j: (i, j, 0)),
        out_shape=jax.ShapeDtypeStruct((b, L, D), F32),
        compiler_params=pltpu.CompilerParams(
            dimension_semantics=("parallel", "parallel"),
            vmem_limit_bytes=VMEM_LIMIT_BYTES),
        name="peer_epilogue",
    )(x, peT, g2, lng, lnb)


def _mixing_sublayer(xin, sc1, sh1, g1, sc2, sh2, h0f, h0b, two_d, w):
    proj = mod_matmul(xin, sc1, sh1, w["w_in"], tm=min(512, xin.shape[1]), tn=1152)
    xbc = conv_silu(proj, w["conv_w8"], w["conv_b"])
    yf, yb, hf, hb = ssd_scan(xbc, proj, w["a_lanes"], w["bias_lanes"], w["sel"], h0f, h0b)
    mixed = pool_mixed(proj, two_d)
    xo, hT = mixer_epilogue(yf, yb, xbc, proj, mixed, xin, w["dskip"], w["normg"], w["w_ssd_out"],
                            w["pool_w"], w["pscale"], w["w_o"], g1, w["ln_g0"], w["ln_b0"], sc2, sh2)
    return xo, hT, hf, hb


def _peer_sublayer(xin, hT, g2, w):
    l1w, e1w, r2, e2 = peer_select(hT, w["wqT"], w["keys"])
    peT = peer_dense(hT, w["u"], w["vT"], l1w, e1w, r2, e2)
    return peer_epilogue(xin, peT, g2, w["ln_g1"], w["ln_b1"])


def kernel(x, c, ctx, c_ctx, w_mod, b_mod, w_in, conv_w, conv_b, a_log, dt_bias, d_skip,
           ssd_norm_g, w_ssd_out, pool_w, pool_scale, w_o, ln_g, ln_b, w_q, sub_keys, u_tab, v_tab):
    B = x.shape[0]
    silu_c = jax.nn.silu(c)
    silu_cc = jax.nn.silu(c_ctx)
    mod_in = jnp.pad(jnp.concatenate([silu_c, silu_cc[None]], axis=0), ((0, 8 - (B + 1)), (0, 0)))
    ones_b = jnp.ones((B, 1, 1), F32)
    pad_lanes = jnp.zeros((LANES - 2 * SSD_HEADS,), F32)
    state0 = jnp.zeros((B, SSD_PAIRS, SSD_STATE, PAIR_W), F32)
    sel = head_lane_selector()
    for l in range(DEPTH):
        last = l == DEPTH - 1
        mod_all = matmul(mod_in, w_mod[l].astype(BF16), tn=1536) + b_mod[l]
        sh1, sc1, g1, sh2, sc2, g2 = jnp.split(mod_all[:B, None, :], 6, axis=-1)
        mc = [m[None, None, :] * ones_b for m in jnp.split(mod_all[B], 6)]
        w = dict(
            w_in=pack_w_in(w_in[l]),
            conv_w8=jnp.pad(conv_w[l], ((0, 8 - CONV_W), (0, 0))),
            conv_b=conv_b[l][None],
            a_lanes=jnp.concatenate([a_log[l, 0], a_log[l, 1], pad_lanes])[None],
            bias_lanes=jnp.concatenate([dt_bias[l, 0], dt_bias[l, 1], pad_lanes])[None],
            sel=sel,
            dskip=jnp.repeat(d_skip[l], SSD_HEAD_DIM)[None],
            normg=ssd_norm_g[l][None],
            w_ssd_out=w_ssd_out[l].astype(BF16),
            pool_w=pool_w[l].astype(BF16),
            pscale=pool_scale[l][None],
            w_o=w_o[l].astype(BF16),
            ln_g0=ln_g[l, 0][None], ln_b0=ln_b[l, 0][None],
            ln_g1=ln_g[l, 1][None], ln_b1=ln_b[l, 1][None],
            wqT=w_q[l].T.astype(BF16),
            keys=sub_keys[l].reshape(2 * PEER_HEADS, N_KEYS, D_HALF).astype(BF16),
            u=u_tab[l].astype(BF16),
            vT=v_tab[l].T.astype(BF16),
        )
        ctx_mix, hT_c, hf_c, hb_c = _mixing_sublayer(ctx, mc[1], mc[0], mc[2], mc[4], mc[3],
                                                     state0, state0, False, w)
        x, hT, _, _ = _mixing_sublayer(x, sc1, sh1, g1, sc2, sh2, hf_c, hb_c, True, w)
        x = _peer_sublayer(x, hT, g2, w)
        if not last:
            ctx = _peer_sublayer(ctx_mix, hT_c, mc[5], w)
    return x
```

```python
import functools

import jax
import jax.numpy as jnp
from jax import lax
from jax.experimental import pallas as pl
from jax.experimental.pallas import tpu as pltpu

D_MODEL = 1024
DEPTH = 4
GRID_W = 64
DEEPNORM_ALPHA = (2.0 * DEPTH) ** 0.25
LN_EPS = 1e-6

SSD_HEAD_DIM = 64
D_SSD = D_MODEL
SSD_HEADS = D_SSD // SSD_HEAD_DIM
SSD_GROUPS = 4
SSD_HPG = SSD_HEADS // SSD_GROUPS
SSD_STATE = 128
SSD_CHUNK = 128
CONV_W = 5

POOL_GROUPS = 4
POOL_CH = D_MODEL // 8
POOL_WINDOWS = (2, 4, 8, 16)

PEER_HEADS = 8
N_KEYS = 128
N_EXPERTS = N_KEYS * N_KEYS
PEER_TOPK = 16
PEER_QDIM = 256
D_HALF = PEER_QDIM // 2

COL_B = D_SSD
COL_C = COL_B + SSD_GROUPS * SSD_STATE
COL_DTF = COL_C + SSD_GROUPS * SSD_STATE
COL_DTB = COL_DTF + SSD_HEADS
SCAN_COLS = COL_DTB + SSD_HEADS
COL_Z = SCAN_COLS
COL_POOL = COL_Z + D_SSD
COL_GP = COL_POOL + POOL_GROUPS * POOL_CH
COL_GS = COL_GP + D_MODEL
W_IN_COLS = COL_GS + D_MODEL

F32 = jnp.float32
BF16 = jnp.bfloat16

LANES = 128
VMEM_LIMIT_BYTES = 56 * 1024 * 1024


def _mod_matmul_kernel(a_ref, sc_ref, sh_ref, w_ref, o_ref):
    a = a_ref[0] * (1.0 + sc_ref[0]) + sh_ref[0]
    o_ref[0] = jnp.dot(a.astype(BF16), w_ref[...], preferred_element_type=F32)


def mod_matmul(a, scale, shift, w, tm=512, tn=None):
    b, L, K = a.shape
    N = w.shape[1]
    if tn is None:
        tn = N
    assert L % tm == 0 and N % tn == 0
    return pl.pallas_call(
        _mod_matmul_kernel,
        grid=(b, L // tm, N // tn),
        in_specs=[
            pl.BlockSpec((1, tm, K), lambda i, j, k: (i, j, 0)),
            pl.BlockSpec((1, 1, K), lambda i, j, k: (i, 0, 0)),
            pl.BlockSpec((1, 1, K), lambda i, j, k: (i, 0, 0)),
            pl.BlockSpec((K, tn), lambda i, j, k: (0, k)),
        ],
        out_specs=pl.BlockSpec((1, tm, tn), lambda i, j, k: (i, j, k)),
        out_shape=jax.ShapeDtypeStruct((b, L, N), F32),
        compiler_params=pltpu.CompilerParams(
            dimension_semantics=("parallel", "parallel", "arbitrary"),
            vmem_limit_bytes=VMEM_LIMIT_BYTES),
        name="mod_matmul",
    )(a, scale, shift, w)


def _matmul_kernel(a_ref, w_ref, o_ref):
    o_ref[...] = jnp.dot(a_ref[...].astype(BF16), w_ref[...], preferred_element_type=F32)


def matmul(a, w, tm=512, tn=None):
    M, K = a.shape
    N = w.shape[1]
    if tn is None:
        tn = N
    tm = min(tm, M)
    assert M % tm == 0 and N % tn == 0
    return pl.pallas_call(
        _matmul_kernel,
        grid=(M // tm, N // tn),
        in_specs=[
            pl.BlockSpec((tm, K), lambda i, k: (i, 0)),
            pl.BlockSpec((K, tn), lambda i, k: (0, k)),
        ],
        out_specs=pl.BlockSpec((tm, tn), lambda i, k: (i, k)),
        out_shape=jax.ShapeDtypeStruct((M, N), F32),
        compiler_params=pltpu.CompilerParams(
            dimension_semantics=("parallel", "arbitrary"),
            vmem_limit_bytes=VMEM_LIMIT_BYTES),
        name="matmul",
    )(a, w)


SEL_TOK = 256
NOT_RANKED = 127.0


def _dup_bf16_words(x):
    hi = lax.bitcast_convert_type(x.astype(BF16).astype(F32), jnp.uint32)
    return hi | (hi >> 16)


def _top16(s, vals_ref, want_rank):
    rank = jnp.full(s.shape, NOT_RANKED, F32) if want_rank else None
    for r in range(PEER_TOPK):
        m = jnp.max(s, axis=0, keepdims=True)
        eq = s == m
        if want_rank:
            rank = jnp.where(eq, float(r), rank)
        s = jnp.where(eq, -jnp.inf, s)
        vals_ref[r:r + 1, :] = m
    return rank


def _extract_max_by_index(vals, order):
    m = jnp.max(vals, axis=0, keepdims=True)
    first = jnp.min(jnp.where(vals == m, order, jnp.int32(2 ** 30)), axis=0, keepdims=True)
    return m, order == first


def _top16_by_index(s, vals_ref):
    order = lax.broadcasted_iota(jnp.int32, s.shape, 0)
    rank = jnp.full(s.shape, NOT_RANKED, F32)
    for r in range(PEER_TOPK):
        m, hit = _extract_max_by_index(s, order)
        rank = jnp.where(hit, float(r), rank)
        s = jnp.where(hit, -jnp.inf, s)
        vals_ref[r:r + 1, :] = m
    return rank


CAND_ROWS = 16 + 7 * 8 + 8


def _candidates(v1, v2):
    slabs = [v1[0:1] + v2]
    for a in range(1, 8):
        slabs.append(v1[a:a + 1] + v2[0:8])
    slabs.append(v1[8:16] + v2[0:1])
    return jnp.concatenate(slabs, axis=0)


def _candidate_flat_index(t):
    r = lax.broadcasted_iota(jnp.int32, (CAND_ROWS, t), 0)
    mid = 16 * (1 + (r - 16) // 8) + (r - 16) % 8
    return jnp.where(r < 16, r, jnp.where(r < 72, mid, 16 * (r - 64)))


def _counts_per_first_rank(self):
    counts = [jnp.sum(self[0:16], axis=0, keepdims=True)]
    for a in range(1, 8):
        counts.append(jnp.sum(self[8 + 8 * a:16 + 8 * a], axis=0, keepdims=True))
    for a in range(8, 16):
        counts.append(self[64 + a:65 + a])
    return counts


def _write_selection(h, s1, s2, v1, v2, l1, rank2, self, cand, refs):
    l1w_ref, e1w_ref, r2_ref, e2_ref = refs
    top = v1[0:1] + v2[0:1]
    z = jnp.sum(self * jnp.exp(cand - top), axis=0, keepdims=True)
    e1 = jnp.exp(s1 - v1[0:1])
    e2 = jnp.exp(s2 - v2[0:1]) / z
    l1w_ref[h] = _dup_bf16_words(l1)
    e1w_ref[h] = _dup_bf16_words(e1)
    r2_ref[h] = rank2.astype(BF16)
    e2_ref[h] = e2.astype(BF16)


def _count_rows(mask):
    return jnp.sum(jnp.where(mask, 1.0, 0.0), axis=0, keepdims=True)


def _peer_select_kernel(hT_ref, wqT_ref, keys_ref, l1w_ref, e1w_ref, r2_ref, e2_ref,
                        q_scr, v1_scr, v2_scr):
    q_scr[...] = jnp.dot(wqT_ref[...], hT_ref[...], preferred_element_type=F32).astype(BF16)
    out_refs = (l1w_ref, e1w_ref, r2_ref, e2_ref)
    k = float(PEER_TOPK)

    def scores(h):
        off = pl.multiple_of(h * PEER_QDIM, PEER_QDIM)
        s1 = jnp.dot(keys_ref[2 * h], q_scr[pl.ds(off, D_HALF), :], preferred_element_type=F32)
        s2 = jnp.dot(keys_ref[2 * h + 1], q_scr[pl.ds(off + D_HALF, D_HALF), :],
                     preferred_element_type=F32)
        return s1, s2

    def head(h, tied):
        s1, s2 = scores(h)
        _top16(s1, v1_scr, False)
        rank2 = _top16(s2, v2_scr, True)
        v1 = v1_scr[...]
        v2 = v2_scr[...]
        cand = _candidates(v1, v2)
        rest = cand
        tau = None
        for r in range(PEER_TOPK):
            tau = jnp.max(rest, axis=0, keepdims=True)
            if r < PEER_TOPK - 1:
                rest = jnp.where(rest == tau, -jnp.inf, rest)
        self = jnp.where(cand >= tau, 1.0, 0.0)
        counts = _counts_per_first_rank(self)
        l1 = jnp.zeros(s1.shape, F32)
        for a in range(PEER_TOPK):
            l1 = jnp.where(s1 == v1[a:a + 1], counts[a], l1)
        _write_selection(h, s1, s2, v1, v2, l1, rank2, self, cand, out_refs)
        bad = ((_count_rows(s1 >= v1[15:16]) != k) | (_count_rows(s2 >= v2[15:16]) != k)
               | (jnp.sum(self, axis=0, keepdims=True) != k))
        return jnp.maximum(tied, jnp.where(bad, 1.0, 0.0))

    tied = lax.fori_loop(0, PEER_HEADS, head, jnp.zeros((1, SEL_TOK), F32), unroll=4)

    @pl.when(jnp.max(tied) > 0.0)
    def _():
        def head_by_index(h, carry):
            s1, s2 = scores(h)
            rank1 = _top16_by_index(s1, v1_scr)
            rank2 = _top16_by_index(s2, v2_scr)
            v1 = v1_scr[...]
            v2 = v2_scr[...]
            cand = _candidates(v1, v2)
            order = _candidate_flat_index(SEL_TOK)
            rest = cand
            self = jnp.zeros(cand.shape, F32)
            for r in range(PEER_TOPK):
                _, hit = _extract_max_by_index(rest, order)
                self = jnp.where(hit, 1.0, self)
                rest = jnp.where(hit, -jnp.inf, rest)
            counts = _counts_per_first_rank(self)
            l1 = jnp.zeros(s1.shape, F32)
            for a in range(PEER_TOPK):
                l1 = jnp.where(rank1 == float(a), counts[a], l1)
            _write_selection(h, s1, s2, v1, v2, l1, rank2, self, cand, out_refs)
            return carry

        lax.fori_loop(0, PEER_HEADS, head_by_index, 0)


def peer_select(hT, wqT_bf, keys_bf):
    D, N = hT.shape
    assert N % SEL_TOK == 0
    H = PEER_HEADS
    blk = pl.BlockSpec((H, N_KEYS, SEL_TOK), lambda t: (0, 0, t))
    return pl.pallas_call(
        _peer_select_kernel,
        grid=(N // SEL_TOK,),
        in_specs=[
            pl.BlockSpec((D, SEL_TOK), lambda t: (0, t)),
            pl.BlockSpec((H * PEER_QDIM, D), lambda t: (0, 0)),
            pl.BlockSpec((2 * H, N_KEYS, D_HALF), lambda t: (0, 0, 0)),
        ],
        out_specs=[blk, blk, blk, blk],
        out_shape=[
            jax.ShapeDtypeStruct((H, N_KEYS, N), jnp.uint32),
            jax.ShapeDtypeStruct((H, N_KEYS, N), jnp.uint32),
            jax.ShapeDtypeStruct((H, N_KEYS, N), BF16),
            jax.ShapeDtypeStruct((H, N_KEYS, N), BF16),
        ],
        scratch_shapes=[
            pltpu.VMEM((H * PEER_QDIM, SEL_TOK), BF16),
            pltpu.VMEM((PEER_TOPK, SEL_TOK), F32),
            pltpu.VMEM((PEER_TOPK, SEL_TOK), F32),
        ],
        compiler_params=pltpu.CompilerParams(
            dimension_semantics=("parallel",),
            vmem_limit_bytes=VMEM_LIMIT_BYTES),
        name="peer_select",
    )(hT, wqT_bf, keys_bf)


PEER_TOK = 512
PEER_EC = 1024
PEER_LANES = 256
BF16_ROWS = 16

LOG2_E = 1.4426950408889634
GELU_A = -2.0 * 0.7978845608028654 * LOG2_E
GELU_B = GELU_A * 0.044715


def _gelu_tanh(x):
    return x / (1.0 + jnp.exp2(x * (GELU_A + GELU_B * (x * x))))


N_CHUNKS = N_EXPERTS // PEER_EC
PIPE_DEPTH = 2


N_TILE_ROWS = PEER_EC // N_KEYS
N_LANE_GROUPS = PEER_TOK // PEER_LANES
MXU_K = 256
MXU_N = 256


def _peer_gate_rows(chunk, il, lane_groups, s_in, w_out, l1w_ref, e1w_ref, r2_ref, e2_ref):
    n_sub = N_KEYS // BF16_ROWS
    i = chunk * N_TILE_ROWS + il
    for lg in lane_groups:
        lanes = slice(lg * PEER_LANES, (lg + 1) * PEER_LANES)
        gs = [jnp.zeros((BF16_ROWS, PEER_LANES), BF16) for _ in range(n_sub)]
        for h in range(PEER_HEADS):
            l1 = pltpu.bitcast(jnp.broadcast_to(l1w_ref[h, pl.ds(i, 1), lanes], (8, PEER_LANES)), BF16)
            e1 = pltpu.bitcast(jnp.broadcast_to(e1w_ref[h, pl.ds(i, 1), lanes], (8, PEER_LANES)), BF16)
            for sub in range(n_sub):
                rows = slice(sub * BF16_ROWS, (sub + 1) * BF16_ROWS)
                gate = e2_ref[h, rows, lanes] * e1
                gs[sub] = gs[sub] + jnp.where(r2_ref[h, rows, lanes] < l1, gate, jnp.zeros_like(gate))
        for sub in range(n_sub):
            rows = slice(il * N_KEYS + sub * BF16_ROWS, il * N_KEYS + (sub + 1) * BF16_ROWS)
            act = _gelu_tanh(s_in[rows, lanes]).astype(BF16)
            w_out[rows, lanes] = gs[sub] * act


def _peer_kernel(xT_ref, u_ref, vT_ref, l1w_ref, e1w_ref, r2_ref, e2_ref, o_ref,
                 s_a, s_b, w_a, w_b, acc_scr, *, n_work):
    s = pl.program_id(0)

    @pl.when(s == 0)
    def _():
        s_a[...] = jnp.zeros_like(s_a)
        s_b[...] = jnp.zeros_like(s_b)
        w_a[...] = jnp.zeros_like(w_a)
        w_b[...] = jnp.zeros_like(w_b)

    drain = s - PIPE_DEPTH
    drain_chunk = drain % N_CHUNKS

    @pl.when((drain_chunk == 0) | (s == 0))
    def _():
        acc_scr[...] = jnp.zeros_like(acc_scr)

    gate_chunk = jnp.clip(s - 1, 0, n_work - 1) % N_CHUNKS

    def step(s_out, s_in, w_out, w_in):
        d_model = xT_ref.shape[0]
        lane_tiles = range(PEER_TOK // MXU_N)
        pieces1 = [(n, k) for n in lane_tiles for k in range(d_model // MXU_K)]
        pieces2 = [(n, k) for n in lane_tiles for k in range(PEER_EC // MXU_K)]
        assert len(pieces2) == N_TILE_ROWS and N_TILE_ROWS % len(pieces1) == 0
        every = N_TILE_ROWS // len(pieces1)
        gate_refs = (s_in, w_out, l1w_ref, e1w_ref, r2_ref, e2_ref)
        tile = lambda n, k: (slice(n * MXU_N, (n + 1) * MXU_N), slice(k * MXU_K, (k + 1) * MXU_K))
        for il in range(N_TILE_ROWS):
            part1 = None
            if il % every == 0:
                lanes1, ks1 = tile(*pieces1[il // every])
                part1 = jnp.dot(u_ref[:, ks1], xT_ref[ks1, lanes1], preferred_element_type=F32)
            lanes2, ks2 = tile(*pieces2[il])
            part2 = jnp.dot(vT_ref[:, ks2], w_in[ks2, lanes2], preferred_element_type=F32)
            _peer_gate_rows(gate_chunk, il, range(N_LANE_GROUPS), *gate_refs)
            if part1 is not None:
                if pieces1[il // every][1] == 0:
                    s_out[:, lanes1] = part1
                else:
                    s_out[:, lanes1] += part1
            acc_scr[:, lanes2] += part2

    @pl.when(s % 2 == 0)
    def _():
        step(s_a, s_b, w_b, w_a)

    @pl.when(s % 2 == 1)
    def _():
        step(s_b, s_a, w_a, w_b)

    @pl.when((drain >= 0) & (drain_chunk == N_CHUNKS - 1))
    def _():
        o_ref[...] = acc_scr[...]


def peer_dense(xT, u_bf, vT_bf, l1w, e1w, r2, e2):
    D, N = xT.shape
    assert N % PEER_TOK == 0
    H = PEER_HEADS
    n_work = (N // PEER_TOK) * N_CHUNKS
    fill = lambda s: jnp.minimum(s, n_work - 1)
    gate = lambda s: jnp.clip(s - 1, 0, n_work - 1)
    drain = lambda s: jnp.maximum(s - PIPE_DEPTH, 0)
    sel_blk = pl.BlockSpec((H, N_KEYS, PEER_TOK), lambda s: (0, 0, gate(s) // N_CHUNKS))
    return pl.pallas_call(
        functools.partial(_peer_kernel, n_work=n_work),
        grid=(n_work + PIPE_DEPTH,),
        in_specs=[
            pl.BlockSpec((D, PEER_TOK), lambda s: (0, fill(s) // N_CHUNKS)),
            pl.BlockSpec((PEER_EC, D), lambda s: (fill(s) % N_CHUNKS, 0)),
            pl.BlockSpec((D, PEER_EC), lambda s: (0, drain(s) % N_CHUNKS)),
            sel_blk, sel_blk, sel_blk, sel_blk,
        ],
        out_specs=pl.BlockSpec((D, PEER_TOK), lambda s: (0, drain(s) // N_CHUNKS)),
        out_shape=jax.ShapeDtypeStruct((D, N), F32),
        scratch_shapes=[
            pltpu.VMEM((PEER_EC, PEER_TOK), F32),
            pltpu.VMEM((PEER_EC, PEER_TOK), F32),
            pltpu.VMEM((PEER_EC, PEER_TOK), BF16),
            pltpu.VMEM((PEER_EC, PEER_TOK), BF16),
            pltpu.VMEM((D, PEER_TOK), F32),
        ],
        compiler_params=pltpu.CompilerParams(
            dimension_semantics=("arbitrary",),
            vmem_limit_bytes=VMEM_LIMIT_BYTES),
        name="peer_dense",
    )(xT, u_bf, vT_bf, l1w, e1w, r2, e2)


P_X = 0
P_Z = 1024
P_GP = 2048
P_GS = 3072
P_B = 4096
P_C = 4608
P_POOL = 5120
P_DT = 5632
P_COLS = 5760


def pack_w_in(w):
    pad = jnp.zeros((w.shape[0], LANES - 2 * SSD_HEADS), w.dtype)
    return jnp.concatenate([
        w[:, :COL_B], w[:, COL_Z:COL_POOL], w[:, COL_GP:COL_GS], w[:, COL_GS:],
        w[:, COL_B:COL_C], w[:, COL_C:COL_DTF], w[:, COL_POOL:COL_GP],
        w[:, COL_DTF:SCAN_COLS], pad], axis=1).astype(BF16)


def _sigmoid(x):
    return 1.0 / (1.0 + jnp.exp(-x))


CONV_TL = 256
CONV_TC = 512
HALO = 8


def _conv_kernel(cur_ref, prev_ref, next_ref, w_ref, b_ref, o_ref):
    j = pl.program_id(1)
    nj = pl.num_programs(1)
    prev = jnp.where(j == 0, 0.0, prev_ref[0, 0])
    nxt = jnp.where(j == nj - 1, 0.0, next_ref[0, 0])
    ext = jnp.concatenate([prev, cur_ref[0], nxt], axis=0)
    tl = cur_ref.shape[1]
    acc = jnp.zeros((tl, cur_ref.shape[2]), F32) + b_ref[...]
    for k in range(CONV_W):
        off = HALO + k - CONV_W // 2
        acc = acc + w_ref[k:k + 1, :] * ext[off:off + tl]
    o_ref[0] = acc * _sigmoid(acc)


def conv_silu(proj, conv_w8, conv_b):
    b, L, _ = proj.shape
    tl = min(CONV_TL, L)
    n_c = 2048 // CONV_TC
    proj4 = proj.reshape(b, L // HALO, HALO, P_COLS)
    nh = tl // HALO
    last_h = L // HALO - 1
    colmap = lambda c: jnp.where(c < 2, c, c + (P_B // CONV_TC - 2))
    return pl.pallas_call(
        _conv_kernel,
        grid=(b, L // tl, n_c),
        in_specs=[
            pl.BlockSpec((1, tl, CONV_TC), lambda i, j, c: (i, j, colmap(c))),
            pl.BlockSpec((1, 1, HALO, CONV_TC), lambda i, j, c: (i, jnp.maximum(j * nh - 1, 0), 0, colmap(c))),
            pl.BlockSpec((1, 1, HALO, CONV_TC), lambda i, j, c: (i, jnp.minimum((j + 1) * nh, last_h), 0, colmap(c))),
            pl.BlockSpec((8, CONV_TC), lambda i, j, c: (0, c)),
            pl.BlockSpec((1, CONV_TC), lambda i, j, c: (0, c)),
        ],
        out_specs=pl.BlockSpec((1, tl, CONV_TC), lambda i, j, c: (i, j, c)),
        out_shape=jax.ShapeDtypeStruct((b, L, 2048), F32),
        compiler_params=pltpu.CompilerParams(
            dimension_semantics=("parallel", "parallel", "parallel"),
            vmem_limit_bytes=VMEM_LIMIT_BYTES),
        name="conv_silu",
    )(proj, proj4, proj4, conv_w8, conv_b)


Q = SSD_CHUNK
XB_B = 1024
XB_C = 1536


def _split3(v):
    hi = v.astype(BF16)
    r1 = v - hi.astype(F32)
    mid = r1.astype(BF16)
    lo = (r1 - mid.astype(F32)).astype(BF16)
    return hi, mid, lo


SSD_PAIRS = SSD_HEADS // 2
PAIR_W = 2 * SSD_HEAD_DIM


def head_lane_selector():
    k = jnp.arange(2 * LANES)[:, None] % LANES
    col_head = jnp.arange(D_SSD)[None, :] // SSD_HEAD_DIM
    return jnp.stack([k == col_head, k == col_head + SSD_HEADS]).astype(BF16)


def _ssd_decays(dt_raw, a_lanes, bias_lanes, sel, rev):
    z = dt_raw + bias_lanes
    dt = jnp.maximum(z, 0.0) + jnp.log(1.0 + jnp.exp(-jnp.abs(z)))
    v = dt * (-jnp.exp(a_lanes))
    li = lax.broadcasted_iota(jnp.int32, (Q, Q), 0)
    si = lax.broadcasted_iota(jnp.int32, (Q, Q), 1)
    tri = (si >= li) if rev else (si <= li)
    tri_bf = jnp.where(tri, 1.0, 0.0).astype(BF16)
    hi, mid, lo = _split3(v)
    acum = (jnp.dot(tri_bf, hi, preferred_element_type=F32)
            + jnp.dot(tri_bf, mid, preferred_element_type=F32)
            + jnp.dot(tri_bf, lo, preferred_element_type=F32))
    last = 0 if rev else Q - 1
    tot = acum[last:last + 1, :]
    stack = jnp.concatenate([jnp.exp(acum), jnp.exp(tot - acum) * dt,
                             jnp.broadcast_to(jnp.exp(tot), (8, LANES))], axis=0)
    s_hi = stack.astype(BF16)
    s_mid = (stack - s_hi.astype(F32)).astype(BF16)
    spread = jnp.dot(jnp.concatenate([s_hi, s_mid], axis=1), sel, preferred_element_type=F32)
    return dict(tri=tri, acum=acum, acum_t=acum.T, dt_t=dt.T,
                e_in=spread[0:Q], w_end=spread[Q:2 * Q], e_tot=spread[2 * Q:2 * Q + 1],
                lane0=SSD_HEADS if rev else 0)


def _ssd_group(x_ref, g):
    bg = x_ref[0, :, XB_B + g * SSD_STATE:XB_B + (g + 1) * SSD_STATE]
    cg = x_ref[0, :, XB_C + g * SSD_STATE:XB_C + (g + 1) * SSD_STATE].astype(BF16)
    bg_t = bg.T.astype(BF16)
    cb = jnp.dot(cg, bg_t, preferred_element_type=F32)
    return cg, bg_t, cb


def _ssd_pair(d, grp, x_ref, h_scr, y_ref, pr):
    cg, bg_t, cb = grp
    cols = slice(pr * PAIR_W, (pr + 1) * PAIR_W)
    xh = x_ref[0, :, cols]
    xh_bf = xh.astype(BF16)
    halves = []
    for e in range(2):
        ln = d["lane0"] + 2 * pr + e
        seg = d["acum"][:, ln:ln + 1] - d["acum_t"][ln:ln + 1, :]
        decay = jnp.exp(jnp.where(d["tri"], seg, -jnp.inf))
        wgt = (cb * decay * d["dt_t"][ln:ln + 1, :]).astype(BF16)
        halves.append(jnp.dot(wgt, xh_bf, preferred_element_type=F32))
    first = lax.broadcasted_iota(jnp.int32, (Q, PAIR_W), 1) < SSD_HEAD_DIM
    h_t = h_scr[pr]
    y = (jnp.where(first, halves[0], halves[1])
         + d["e_in"][:, cols] * jnp.dot(cg, h_t.astype(BF16), preferred_element_type=F32))
    y_ref[0, :, cols] = y
    xs = (xh * d["w_end"][:, cols]).astype(BF16)
    h_scr[pr] = d["e_tot"][:, cols] * h_t + jnp.dot(bg_t, xs, preferred_element_type=F32)


def _ssd_kernel(xf_ref, dtf_ref, xb_ref, dtb_ref, a_ref, bias_ref, sel_ref, h0f_ref, h0b_ref,
                yf_ref, yb_ref, hf_ref, hb_ref, hf_scr, hb_scr):
    c = pl.program_id(1)

    @pl.when(c == 0)
    def _():
        hf_scr[...] = h0f_ref[0]
        hb_scr[...] = h0b_ref[0]

    df = _ssd_decays(dtf_ref[0], a_ref[...], bias_ref[...], sel_ref[0], False)
    db = _ssd_decays(dtb_ref[0], a_ref[...], bias_ref[...], sel_ref[1], True)
    pairs_per_group = SSD_HPG // 2
    for g in range(SSD_GROUPS):
        gf = _ssd_group(xf_ref, g)
        gb = _ssd_group(xb_ref, g)
        for r in range(pairs_per_group):
            pr = g * pairs_per_group + r
            _ssd_pair(df, gf, xf_ref, hf_scr, yf_ref, pr)
            _ssd_pair(db, gb, xb_ref, hb_scr, yb_ref, pr)

    @pl.when(c == pl.num_programs(1) - 1)
    def _():
        hf_ref[0] = hf_scr[...]
        hb_ref[0] = hb_scr[...]


def ssd_scan(xbc, proj, a_lanes, bias_lanes, sel, h0f, h0b):
    b, L, _ = xbc.shape
    nc = L // Q
    st_shape = (b, SSD_PAIRS, SSD_STATE, PAIR_W)
    st_spec = pl.BlockSpec((1, SSD_PAIRS, SSD_STATE, PAIR_W), lambda i, c: (i, 0, 0, 0))
    dt_blk = P_DT // LANES
    return pl.pallas_call(
        _ssd_kernel,
        grid=(b, nc),
        in_specs=[
            pl.BlockSpec((1, Q, 2048), lambda i, c: (i, c, 0)),
            pl.BlockSpec((1, Q, LANES), lambda i, c: (i, c, dt_blk)),
            pl.BlockSpec((1, Q, 2048), lambda i, c: (i, nc - 1 - c, 0)),
            pl.BlockSpec((1, Q, LANES), lambda i, c: (i, nc - 1 - c, dt_blk)),
            pl.BlockSpec((1, LANES), lambda i, c: (0, 0)),
            pl.BlockSpec((1, LANES), lambda i, c: (0, 0)),
            pl.BlockSpec((2, 2 * LANES, D_SSD), lambda i, c: (0, 0, 0)),
            st_spec, st_spec,
        ],
        out_specs=[
            pl.BlockSpec((1, Q, D_SSD), lambda i, c: (i, c, 0)),
            pl.BlockSpec((1, Q, D_SSD), lambda i, c: (i, nc - 1 - c, 0)),
            st_spec, st_spec,
        ],
        out_shape=[
            jax.ShapeDtypeStruct((b, L, D_SSD), F32),
            jax.ShapeDtypeStruct((b, L, D_SSD), F32),
            jax.ShapeDtypeStruct(st_shape, F32),
            jax.ShapeDtypeStruct(st_shape, F32),
        ],
        scratch_shapes=[
            pltpu.VMEM((SSD_PAIRS, SSD_STATE, PAIR_W), F32),
            pltpu.VMEM((SSD_PAIRS, SSD_STATE, PAIR_W), F32),
        ],
        compiler_params=pltpu.CompilerParams(
            dimension_semantics=("parallel", "arbitrary"),
            vmem_limit_bytes=VMEM_LIMIT_BYTES),
        name="ssd_scan",
    )(xbc, proj, xbc, proj, a_lanes, bias_lanes, sel, h0f, h0b)


POOL_TILE = 512


def _box_offsets(w):
    return w // 2, w - 1 - w // 2


def _pool_group(u_ref, o_ref, pad_scr, r_scr, w, L, row_len, two_d):
    lo, hi = _box_offsets(w)
    tile = min(POOL_TILE, L)
    n_tiles = L // tile
    half = (pad_scr.shape[0] - L) // 2

    if two_d:
        n_rows = L // row_len
        pad_scr[0:half, :] = jnp.zeros((half, LANES), F32)
        pad_scr[half + L:, :] = jnp.zeros((half, LANES), F32)
        pad_scr[half:half + L, :] = u_ref[0]

        def rows_body(i, carry):
            base = pl.multiple_of(i * tile, tile)
            acc = jnp.zeros((tile, LANES), F32)
            for k in range(-lo, hi + 1):
                acc = acc + pad_scr[pl.ds(base + (half + k * row_len), tile), :]
            row = (base + lax.broadcasted_iota(jnp.int32, (tile, LANES), 0)) // row_len
            cnt = jnp.minimum(row + hi, n_rows - 1) - jnp.maximum(row - lo, 0) + 1
            r_scr[pl.ds(base, tile), :] = acc / cnt.astype(F32)
            return carry

        lax.fori_loop(0, n_tiles, rows_body, 0)

    def cols_body(i, carry):
        base = pl.multiple_of(i * tile, tile)
        u = u_ref[0, pl.ds(base, tile), :]
        t = r_scr[pl.ds(base, tile), :] if two_d else u
        col = lax.broadcasted_iota(jnp.int32, (tile, LANES), 0) % row_len
        acc = t
        for k in range(-lo, hi + 1):
            if k == 0:
                continue
            sh = pltpu.roll(t, (-k) % tile, axis=0)
            ok = (col + k >= 0) & (col + k < row_len)
            acc = acc + jnp.where(ok, sh, 0.0)
        cnt = jnp.minimum(col + hi, row_len - 1) - jnp.maximum(col - lo, 0) + 1
        o_ref[0, pl.ds(base, tile), :] = acc / cnt.astype(F32) - u
        return carry

    lax.fori_loop(0, n_tiles, cols_body, 0)


def _pool_kernel(u_ref, o_ref, pad_scr, r_scr, *, L, row_len, two_d):
    g = pl.program_id(1)
    for gi, w in enumerate(POOL_WINDOWS):
        @pl.when(g == gi)
        def _(w=w):
            _pool_group(u_ref, o_ref, pad_scr, r_scr, w, L, row_len, two_d)


def pool_mixed(proj, two_d):
    b, L, _ = proj.shape
    row_len = GRID_W if two_d else L
    pad_rows = 2 * (max(POOL_WINDOWS) // 2) * row_len if two_d else 2 * HALO
    blk0 = P_POOL // LANES
    return pl.pallas_call(
        functools.partial(_pool_kernel, L=L, row_len=row_len, two_d=two_d),
        grid=(b, POOL_GROUPS),
        in_specs=[pl.BlockSpec((1, L, LANES), lambda i, g: (i, 0, blk0 + g))],
        out_specs=pl.BlockSpec((1, L, LANES), lambda i, g: (i, 0, g)),
        out_shape=jax.ShapeDtypeStruct((b, L, POOL_GROUPS * POOL_CH), F32),
        scratch_shapes=[
            pltpu.VMEM((L + pad_rows, LANES), F32),
            pltpu.VMEM((L, LANES), F32),
        ],
        compiler_params=pltpu.CompilerParams(
            dimension_semantics=("parallel", "parallel"),
            vmem_limit_bytes=VMEM_LIMIT_BYTES),
        name="pool_mixed",
    )(proj)


MIX_TM = 256
RMS_GROUP = D_SSD // SSD_GROUPS


def _layer_norm_rows(v, g, b):
    mu = jnp.mean(v, axis=-1, keepdims=True)
    d = v - mu
    var = jnp.mean(d * d, axis=-1, keepdims=True)
    return d * lax.rsqrt(var + LN_EPS) * g + b


def _mixer_kernel(yf_ref, yb_ref, xh_ref, z_ref, gp_ref, gs_ref, mixed_ref, x_ref,
                  dskip_ref, normg_ref, wssd_ref, poolw_ref, pscale_ref, wo_ref,
                  g1_ref, lng_ref, lnb_ref, sc2_ref, sh2_ref, xo_ref, hT_ref):
    y = yf_ref[0] + yb_ref[0] + dskip_ref[...] * xh_ref[0]
    z = z_ref[0]
    yg = y * (z * _sigmoid(z))
    parts = []
    for g in range(SSD_GROUPS):
        blk = yg[:, g * RMS_GROUP:(g + 1) * RMS_GROUP]
        ms = jnp.mean(blk * blk, axis=-1, keepdims=True)
        parts.append(blk * lax.rsqrt(ms + LN_EPS))
    yn = jnp.concatenate(parts, axis=-1) * normg_ref[...]
    y_s = jnp.dot(yn.astype(BF16), wssd_ref[...], preferred_element_type=F32)
    pooled = []
    for g in range(POOL_GROUPS):
        m = mixed_ref[0][:, g * POOL_CH:(g + 1) * POOL_CH].astype(BF16)
        pooled.append(jnp.dot(m, poolw_ref[g], preferred_element_type=F32))
    y_pool = jnp.concatenate(pooled, axis=-1) * pscale_ref[...]
    merged = _sigmoid(gp_ref[0]) * y_pool + _sigmoid(gs_ref[0]) * y_s
    mix = jnp.dot(merged.astype(BF16), wo_ref[...], preferred_element_type=F32)
    xn = _layer_norm_rows(DEEPNORM_ALPHA * x_ref[0] + g1_ref[0] * mix, lng_ref[...], lnb_ref[...])
    xo_ref[0] = xn
    h2 = xn * (1.0 + sc2_ref[0]) + sh2_ref[0]
    hT_ref[...] = h2.T.astype(BF16)


def mixer_epilogue(yf, yb, xbc, proj, mixed, x, dskip, normg, wssd_bf, poolw_bf, pscale, wo_bf,
                   g1, lng, lnb, sc2, sh2):
    b, L, D = x.shape
    tm = min(MIX_TM, L)
    nj = L // tm
    tok = lambda blk: pl.BlockSpec((1, tm, D), lambda i, j, blk=blk: (i, j, blk))
    vec = pl.BlockSpec((1, D), lambda i, j: (0, 0))
    bvec = pl.BlockSpec((1, 1, D), lambda i, j: (i, 0, 0))
    full2 = pl.BlockSpec((D, D), lambda i, j: (0, 0))
    return pl.pallas_call(
        _mixer_kernel,
        grid=(b, nj),
        in_specs=[
            tok(0), tok(0), tok(0),
            tok(P_Z // D), tok(P_GP // D), tok(P_GS // D),
            pl.BlockSpec((1, tm, POOL_GROUPS * POOL_CH), lambda i, j: (i, j, 0)),
            tok(0),
            vec, vec, full2,
            pl.BlockSpec((POOL_GROUPS, POOL_CH, D // POOL_GROUPS), lambda i, j: (0, 0, 0)),
            vec, full2,
            bvec, vec, vec, bvec, bvec,
        ],
        out_specs=[
            pl.BlockSpec((1, tm, D), lambda i, j: (i, j, 0)),
            pl.BlockSpec((D, tm), lambda i, j: (0, i * nj + j)),
        ],
        out_shape=[
            jax.ShapeDtypeStruct((b, L, D), F32),
            jax.ShapeDtypeStruct((D, b * L), BF16),
        ],
        compiler_params=pltpu.CompilerParams(
            dimension_semantics=("parallel", "parallel"),
            vmem_limit_bytes=VMEM_LIMIT_BYTES),
        name="mixer_epilogue",
    )(yf, yb, xbc, proj, proj, proj, mixed, x, dskip, normg, wssd_bf, poolw_bf, pscale, wo_bf,
      g1, lng, lnb, sc2, sh2)


def _peer_out_kernel(x_ref, peT_ref, g2_ref, lng_ref, lnb_ref, o_ref):
    v = DEEPNORM_ALPHA * x_ref[0] + g2_ref[0] * peT_ref[...].T
    o_ref[0] = _layer_norm_rows(v, lng_ref[...], lnb_ref[...])


def peer_epilogue(x, peT, g2, lng, lnb):
    b, L, D = x.shape
    tm = min(MIX_TM, L)
    nj = L // tm
    vec = pl.BlockSpec((1, D), lambda i, j: (0, 0))
    return pl.pallas_call(
        _peer_out_kernel,
        grid=(b, nj),
        in_specs=[
            pl.BlockSpec((1, tm, D), lambda i, j: (i, j, 0)),
            pl.BlockSpec((D, tm), lambda i, j: (0, i * nj + j)),
            pl.BlockSpec((1, 1, D), lambda i, j: (i, 0, 0)),
            vec, vec,
        ],
        out_specs=pl.BlockSpec((1, tm, D), lambda i, j: (i, j, 0)),
        out_shape=jax.ShapeDtypeStruct((b, L, D), F32),
        compiler_params=pltpu.CompilerParams(
            dimension_semantics=("parallel", "parallel"),
            vmem_limit_bytes=VMEM_LIMIT_BYTES),
        name="peer_epilogue",
    )(x, peT, g2, lng, lnb)


def _mixing_sublayer(xin, sc1, sh1, g1, sc2, sh2, h0f, h0b, two_d, w):
    proj = mod_matmul(xin, sc1, sh1, w["w_in"], tm=min(512, xin.shape[1]), tn=1152)
    xbc = conv_silu(proj, w["conv_w8"], w["conv_b"])
    yf, yb, hf, hb = ssd_scan(xbc, proj, w["a_lanes"], w["bias_lanes"], w["sel"], h0f, h0b)
    mixed = pool_mixed(proj, two_d)
    xo, hT = mixer_epilogue(yf, yb, xbc, proj, mixed, xin, w["dskip"], w["normg"], w["w_ssd_out"],
                            w["pool_w"], w["pscale"], w["w_o"], g1, w["ln_g0"], w["ln_b0"], sc2, sh2)
    return xo, hT, hf, hb


def _peer_sublayer(xin, hT, g2, w):
    l1w, e1w, r2, e2 = peer_select(hT, w["wqT"], w["keys"])
    peT = peer_dense(hT, w["u"], w["vT"], l1w, e1w, r2, e2)
    return peer_epilogue(xin, peT, g2, w["ln_g1"], w["ln_b1"])


def kernel(x, c, ctx, c_ctx, w_mod, b_mod, w_in, conv_w, conv_b, a_log, dt_bias, d_skip,
           ssd_norm_g, w_ssd_out, pool_w, pool_scale, w_o, ln_g, ln_b, w_q, sub_keys, u_tab, v_tab):
    B = x.shape[0]
    silu_c = jax.nn.silu(c)
    silu_cc = jax.nn.silu(c_ctx)
    mod_in = jnp.pad(jnp.concatenate([silu_c, silu_cc[None]], axis=0), ((0, 8 - (B + 1)), (0, 0)))
    ones_b = jnp.ones((B, 1, 1), F32)
    pad_lanes = jnp.zeros((LANES - 2 * SSD_HEADS,), F32)
    state0 = jnp.zeros((B, SSD_PAIRS, SSD_STATE, PAIR_W), F32)
    sel = head_lane_selector()
    for l in range(DEPTH):
        last = l == DEPTH - 1
        mod_all = matmul(mod_in, w_mod[l].astype(BF16), tn=1536) + b_mod[l]
        sh1, sc1, g1, sh2, sc2, g2 = jnp.split(mod_all[:B, None, :], 6, axis=-1)
        mc = [m[None, None, :] * ones_b for m in jnp.split(mod_all[B], 6)]
        w = dict(
            w_in=pack_w_in(w_in[l]),
            conv_w8=jnp.pad(conv_w[l], ((0, 8 - CONV_W), (0, 0))),
            conv_b=conv_b[l][None],
            a_lanes=jnp.concatenate([a_log[l, 0], a_log[l, 1], pad_lanes])[None],
            bias_lanes=jnp.concatenate([dt_bias[l, 0], dt_bias[l, 1], pad_lanes])[None],
            sel=sel,
            dskip=jnp.repeat(d_skip[l], SSD_HEAD_DIM)[None],
            normg=ssd_norm_g[l][None],
            w_ssd_out=w_ssd_out[l].astype(BF16),
            pool_w=pool_w[l].astype(BF16),
            pscale=pool_scale[l][None],
            w_o=w_o[l].astype(BF16),
            ln_g0=ln_g[l, 0][None], ln_b0=ln_b[l, 0][None],
            ln_g1=ln_g[l, 1][None], ln_b1=ln_b[l, 1][None],
            wqT=w_q[l].T.astype(BF16),
            keys=sub_keys[l].reshape(2 * PEER_HEADS, N_KEYS, D_HALF).astype(BF16),
            u=u_tab[l].astype(BF16),
            vT=v_tab[l].T.astype(BF16),
        )
        ctx_mix, hT_c, hf_c, hb_c = _mixing_sublayer(ctx, mc[1], mc[0], mc[2], mc[4], mc[3],
                                                     state0, state0, False, w)
        x, hT, _, _ = _mixing_sublayer(x, sc1, sh1, g1, sc2, sh2, hf_c, hb_c, True, w)
        x = _peer_sublayer(x, hT, g2, w)
        if not last:
            ctx = _peer_sublayer(ctx_mix, hT_c, mc[5], w)
    return x
```

```python
import functools

import jax
import jax.numpy as jnp
from jax import lax
from jax.experimental import pallas as pl
from jax.experimental.pallas import tpu as pltpu

D_MODEL = 1024
DEPTH = 4
GRID_W = 64
DEEPNORM_ALPHA = (2.0 * DEPTH) ** 0.25
LN_EPS = 1e-6

SSD_HEAD_DIM = 64
D_SSD = D_MODEL
SSD_HEADS = D_SSD // SSD_HEAD_DIM
SSD_GROUPS = 4
SSD_HPG = SSD_HEADS // SSD_GROUPS
SSD_STATE = 128
SSD_CHUNK = 128
CONV_W = 5

POOL_GROUPS = 4
POOL_CH = D_MODEL // 8
POOL_WINDOWS = (2, 4, 8, 16)

PEER_HEADS = 8
N_KEYS = 128
N_EXPERTS = N_KEYS * N_KEYS
PEER_TOPK = 16
PEER_QDIM = 256
D_HALF = PEER_QDIM // 2

COL_B = D_SSD
COL_C = COL_B + SSD_GROUPS * SSD_STATE
COL_DTF = COL_C + SSD_GROUPS * SSD_STATE
COL_DTB = COL_DTF + SSD_HEADS
SCAN_COLS = COL_DTB + SSD_HEADS
COL_Z = SCAN_COLS
COL_POOL = COL_Z + D_SSD
COL_GP = COL_POOL + POOL_GROUPS * POOL_CH
COL_GS = COL_GP + D_MODEL
W_IN_COLS = COL_GS + D_MODEL

F32 = jnp.float32
BF16 = jnp.bfloat16

LANES = 128
VMEM_LIMIT_BYTES = 56 * 1024 * 1024


def _mod_matmul_kernel(a_ref, sc_ref, sh_ref, w_ref, o_ref):
    a = a_ref[0] * (1.0 + sc_ref[0]) + sh_ref[0]
    o_ref[0] = jnp.dot(a.astype(BF16), w_ref[...], preferred_element_type=F32)


def mod_matmul(a, scale, shift, w, tm=512, tn=None):
    b, L, K = a.shape
    N = w.shape[1]
    if tn is None:
        tn = N
    assert L % tm == 0 and N % tn == 0
    return pl.pallas_call(
        _mod_matmul_kernel,
        grid=(b, L // tm, N // tn),
        in_specs=[
            pl.BlockSpec((1, tm, K), lambda i, j, k: (i, j, 0)),
            pl.BlockSpec((1, 1, K), lambda i, j, k: (i, 0, 0)),
            pl.BlockSpec((1, 1, K), lambda i, j, k: (i, 0, 0)),
            pl.BlockSpec((K, tn), lambda i, j, k: (0, k)),
        ],
        out_specs=pl.BlockSpec((1, tm, tn), lambda i, j, k: (i, j, k)),
        out_shape=jax.ShapeDtypeStruct((b, L, N), F32),
        compiler_params=pltpu.CompilerParams(
            dimension_semantics=("parallel", "parallel", "arbitrary"),
            vmem_limit_bytes=VMEM_LIMIT_BYTES),
        name="mod_matmul",
    )(a, scale, shift, w)


def _matmul_kernel(a_ref, w_ref, o_ref):
    o_ref[...] = jnp.dot(a_ref[...].astype(BF16), w_ref[...], preferred_element_type=F32)


def matmul(a, w, tm=512, tn=None):
    M, K = a.shape
    N = w.shape[1]
    if tn is None:
        tn = N
    tm = min(tm, M)
    assert M % tm == 0 and N % tn == 0
    return pl.pallas_call(
        _matmul_kernel,
        grid=(M // tm, N // tn),
        in_specs=[
            pl.BlockSpec((tm, K), lambda i, k: (i, 0)),
            pl.BlockSpec((K, tn), lambda i, k: (0, k)),
        ],
        out_specs=pl.BlockSpec((tm, tn), lambda i, k: (i, k)),
        out_shape=jax.ShapeDtypeStruct((M, N), F32),
        compiler_params=pltpu.CompilerParams(
            dimension_semantics=("parallel", "arbitrary"),
            vmem_limit_bytes=VMEM_LIMIT_BYTES),
        name="matmul",
    )(a, w)


SEL_TOK = 256
NOT_RANKED = 127.0


def _dup_bf16_words(x):
    hi = lax.bitcast_convert_type(x.astype(BF16).astype(F32), jnp.uint32)
    return hi | (hi >> 16)


def _top16(s, vals_ref, want_rank):
    rank = jnp.full(s.shape, NOT_RANKED, F32) if want_rank else None
    for r in range(PEER_TOPK):
        m = jnp.max(s, axis=0, keepdims=True)
        eq = s == m
        if want_rank:
            rank = jnp.where(eq, float(r), rank)
        s = jnp.where(eq, -jnp.inf, s)
        vals_ref[r:r + 1, :] = m
    return rank


def _extract_max_by_index(vals, order):
    m = jnp.max(vals, axis=0, keepdims=True)
    first = jnp.min(jnp.where(vals == m, order, jnp.int32(2 ** 30)), axis=0, keepdims=True)
    return m, order == first


def _top16_by_index(s, vals_ref):
    order = lax.broadcasted_iota(jnp.int32, s.shape, 0)
    rank = jnp.full(s.shape, NOT_RANKED, F32)
    for r in range(PEER_TOPK):
        m, hit = _extract_max_by_index(s, order)
        rank = jnp.where(hit, float(r), rank)
        s = jnp.where(hit, -jnp.inf, s)
        vals_ref[r:r + 1, :] = m
    return rank


CAND_ROWS = 16 + 7 * 8 + 8


def _candidates(v1, v2):
    slabs = [v1[0:1] + v2]
    for a in range(1, 8):
        slabs.append(v1[a:a + 1] + v2[0:8])
    slabs.append(v1[8:16] + v2[0:1])
    return jnp.concatenate(slabs, axis=0)


def _candidate_flat_index(t):
    r = lax.broadcasted_iota(jnp.int32, (CAND_ROWS, t), 0)
    mid = 16 * (1 + (r - 16) // 8) + (r - 16) % 8
    return jnp.where(r < 16, r, jnp.where(r < 72, mid, 16 * (r - 64)))


def _counts_per_first_rank(self):
    counts = [jnp.sum(self[0:16], axis=0, keepdims=True)]
    for a in range(1, 8):
        counts.append(jnp.sum(self[8 + 8 * a:16 + 8 * a], axis=0, keepdims=True))
    for a in range(8, 16):
        counts.append(self[64 + a:65 + a])
    return counts


def _write_selection(h, s1, s2, v1, v2, l1, rank2, self, cand, refs):
    l1w_ref, e1w_ref, r2_ref, e2_ref = refs
    top = v1[0:1] + v2[0:1]
    z = jnp.sum(self * jnp.exp(cand - top), axis=0, keepdims=True)
    e1 = jnp.exp(s1 - v1[0:1])
    e2 = jnp.exp(s2 - v2[0:1]) / z
    l1w_ref[h] = _dup_bf16_words(l1)
    e1w_ref[h] = _dup_bf16_words(e1)
    r2_ref[h] = rank2.astype(BF16)
    e2_ref[h] = e2.astype(BF16)


def _count_rows(mask):
    return jnp.sum(jnp.where(mask, 1.0, 0.0), axis=0, keepdims=True)


def _peer_select_kernel(hT_ref, wqT_ref, keys_ref, l1w_ref, e1w_ref, r2_ref, e2_ref,
                        q_scr, v1_scr, v2_scr, tied_scr):
    q_scr[...] = jnp.dot(wqT_ref[...], hT_ref[...], preferred_element_type=F32).astype(BF16)
    out_refs = (l1w_ref, e1w_ref, r2_ref, e2_ref)
    k = float(PEER_TOPK)

    def scores(h):
        off = pl.multiple_of(h * PEER_QDIM, PEER_QDIM)
        s1 = jnp.dot(keys_ref[2 * h], q_scr[pl.ds(off, D_HALF), :], preferred_element_type=F32)
        s2 = jnp.dot(keys_ref[2 * h + 1], q_scr[pl.ds(off + D_HALF, D_HALF), :],
                     preferred_element_type=F32)
        return s1, s2

    def head(h, carry):
        s1, s2 = scores(h)
        _top16(s1, v1_scr, False)
        rank2 = _top16(s2, v2_scr, True)
        v1 = v1_scr[...]
        v2 = v2_scr[...]
        cand = _candidates(v1, v2)
        rest = cand
        tau = None
        for r in range(PEER_TOPK):
            tau = jnp.max(rest, axis=0, keepdims=True)
            if r < PEER_TOPK - 1:
                rest = jnp.where(rest == tau, -jnp.inf, rest)
        self = jnp.where(cand >= tau, 1.0, 0.0)
        counts = _counts_per_first_rank(self)
        l1 = jnp.zeros(s1.shape, F32)
        for a in range(PEER_TOPK):
            l1 = jnp.where(s1 == v1[a:a + 1], counts[a], l1)
        _write_selection(h, s1, s2, v1, v2, l1, rank2, self, cand, out_refs)
        bad = ((_count_rows(s1 >= v1[15:16]) != k) | (_count_rows(s2 >= v2[15:16]) != k)
               | (jnp.sum(self, axis=0, keepdims=True) != k))
        tied_scr[pl.ds(h, 1), :] = jnp.where(bad, 1.0, 0.0)
        return carry

    lax.fori_loop(0, PEER_HEADS, head, 0, unroll=4)

    def redo_if_tied(h, carry):
        @pl.when(jnp.max(tied_scr[pl.ds(h, 1), :]) > 0.0)
        def _():
            s1, s2 = scores(h)
            rank1 = _top16_by_index(s1, v1_scr)
            rank2 = _top16_by_index(s2, v2_scr)
            v1 = v1_scr[...]
            v2 = v2_scr[...]
            cand = _candidates(v1, v2)
            order = _candidate_flat_index(SEL_TOK)
            rest = cand
            self = jnp.zeros(cand.shape, F32)
            for r in range(PEER_TOPK):
                _, hit = _extract_max_by_index(rest, order)
                self = jnp.where(hit, 1.0, self)
                rest = jnp.where(hit, -jnp.inf, rest)
            counts = _counts_per_first_rank(self)
            l1 = jnp.zeros(s1.shape, F32)
            for a in range(PEER_TOPK):
                l1 = jnp.where(rank1 == float(a), counts[a], l1)
            _write_selection(h, s1, s2, v1, v2, l1, rank2, self, cand, out_refs)
        return carry

    lax.fori_loop(0, PEER_HEADS, redo_if_tied, 0)


def peer_select(hT, wqT_bf, keys_bf):
    D, N = hT.shape
    assert N % SEL_TOK == 0
    H = PEER_HEADS
    blk = pl.BlockSpec((H, N_KEYS, SEL_TOK), lambda t: (0, 0, t))
    return pl.pallas_call(
        _peer_select_kernel,
        grid=(N // SEL_TOK,),
        in_specs=[
            pl.BlockSpec((D, SEL_TOK), lambda t: (0, t)),
            pl.BlockSpec((H * PEER_QDIM, D), lambda t: (0, 0)),
            pl.BlockSpec((2 * H, N_KEYS, D_HALF), lambda t: (0, 0, 0)),
        ],
        out_specs=[blk, blk, blk, blk],
        out_shape=[
            jax.ShapeDtypeStruct((H, N_KEYS, N), jnp.uint32),
            jax.ShapeDtypeStruct((H, N_KEYS, N), jnp.uint32),
            jax.ShapeDtypeStruct((H, N_KEYS, N), BF16),
            jax.ShapeDtypeStruct((H, N_KEYS, N), BF16),
        ],
        scratch_shapes=[
            pltpu.VMEM((H * PEER_QDIM, SEL_TOK), BF16),
            pltpu.VMEM((PEER_TOPK, SEL_TOK), F32),
            pltpu.VMEM((PEER_TOPK, SEL_TOK), F32),
            pltpu.VMEM((H, SEL_TOK), F32),
        ],
        compiler_params=pltpu.CompilerParams(
            dimension_semantics=("parallel",),
            vmem_limit_bytes=VMEM_LIMIT_BYTES),
        name="peer_select",
    )(hT, wqT_bf, keys_bf)


PEER_TOK = 512
PEER_EC = 1024
PEER_LANES = 256
BF16_ROWS = 16

LOG2_E = 1.4426950408889634
GELU_A = -2.0 * 0.7978845608028654 * LOG2_E
GELU_B = GELU_A * 0.044715


def _gelu_tanh(x):
    return x / (1.0 + jnp.exp2(x * (GELU_A + GELU_B * (x * x))))


N_CHUNKS = N_EXPERTS // PEER_EC
PIPE_DEPTH = 2


N_TILE_ROWS = PEER_EC // N_KEYS
N_LANE_GROUPS = PEER_TOK // PEER_LANES
MXU_K = 256
MXU_N = 256


def _peer_gate_rows(chunk, il, lane_groups, s_in, w_out, l1w_ref, e1w_ref, r2_ref, e2_ref):
    n_sub = N_KEYS // BF16_ROWS
    i = chunk * N_TILE_ROWS + il
    for lg in lane_groups:
        lanes = slice(lg * PEER_LANES, (lg + 1) * PEER_LANES)
        gs = [jnp.zeros((BF16_ROWS, PEER_LANES), BF16) for _ in range(n_sub)]
        for h in range(PEER_HEADS):
            l1 = pltpu.bitcast(jnp.broadcast_to(l1w_ref[h, pl.ds(i, 1), lanes], (8, PEER_LANES)), BF16)
            e1 = pltpu.bitcast(jnp.broadcast_to(e1w_ref[h, pl.ds(i, 1), lanes], (8, PEER_LANES)), BF16)
            for sub in range(n_sub):
                rows = slice(sub * BF16_ROWS, (sub + 1) * BF16_ROWS)
                gate = e2_ref[h, rows, lanes] * e1
                gs[sub] = gs[sub] + jnp.where(r2_ref[h, rows, lanes] < l1, gate, jnp.zeros_like(gate))
        for sub in range(n_sub):
            rows = slice(il * N_KEYS + sub * BF16_ROWS, il * N_KEYS + (sub + 1) * BF16_ROWS)
            act = _gelu_tanh(s_in[rows, lanes]).astype(BF16)
            w_out[rows, lanes] = gs[sub] * act


def _peer_kernel(xT_ref, u_ref, vT_ref, l1w_ref, e1w_ref, r2_ref, e2_ref, o_ref,
                 s_a, s_b, w_a, w_b, acc_scr, *, n_work):
    s = pl.program_id(0)

    @pl.when(s == 0)
    def _():
        s_a[...] = jnp.zeros_like(s_a)
        s_b[...] = jnp.zeros_like(s_b)
        w_a[...] = jnp.zeros_like(w_a)
        w_b[...] = jnp.zeros_like(w_b)

    drain = s - PIPE_DEPTH
    drain_chunk = drain % N_CHUNKS

    @pl.when((drain_chunk == 0) | (s == 0))
    def _():
        acc_scr[...] = jnp.zeros_like(acc_scr)

    gate_chunk = jnp.clip(s - 1, 0, n_work - 1) % N_CHUNKS

    def step(s_out, s_in, w_out, w_in):
        d_model = xT_ref.shape[0]
        lane_tiles = range(PEER_TOK // MXU_N)
        pieces1 = [(n, k) for n in lane_tiles for k in range(d_model // MXU_K)]
        pieces2 = [(n, k) for n in lane_tiles for k in range(PEER_EC // MXU_K)]
        assert len(pieces2) == N_TILE_ROWS and N_TILE_ROWS % len(pieces1) == 0
        every = N_TILE_ROWS // len(pieces1)
        gate_refs = (s_in, w_out, l1w_ref, e1w_ref, r2_ref, e2_ref)
        tile = lambda n, k: (slice(n * MXU_N, (n + 1) * MXU_N), slice(k * MXU_K, (k + 1) * MXU_K))
        for il in range(N_TILE_ROWS):
            part1 = None
            if il % every == 0:
                lanes1, ks1 = tile(*pieces1[il // every])
                part1 = jnp.dot(u_ref[:, ks1], xT_ref[ks1, lanes1], preferred_element_type=F32)
            lanes2, ks2 = tile(*pieces2[il])
            part2 = jnp.dot(vT_ref[:, ks2], w_in[ks2, lanes2], preferred_element_type=F32)
            _peer_gate_rows(gate_chunk, il, range(N_LANE_GROUPS), *gate_refs)
            if part1 is not None:
                if pieces1[il // every][1] == 0:
                    s_out[:, lanes1] = part1
                else:
                    s_out[:, lanes1] += part1
            acc_scr[:, lanes2] += part2

    @pl.when(s % 2 == 0)
    def _():
        step(s_a, s_b, w_b, w_a)

    @pl.when(s % 2 == 1)
    def _():
        step(s_b, s_a, w_a, w_b)

    @pl.when((drain >= 0) & (drain_chunk == N_CHUNKS - 1))
    def _():
        o_ref[...] = acc_scr[...]


def peer_dense(xT, u_bf, vT_bf, l1w, e1w, r2, e2):
    D, N = xT.shape
    assert N % PEER_TOK == 0
    H = PEER_HEADS
    n_work = (N // PEER_TOK) * N_CHUNKS
    fill = lambda s: jnp.minimum(s, n_work - 1)
    gate = lambda s: jnp.clip(s - 1, 0, n_work - 1)
    drain = lambda s: jnp.maximum(s - PIPE_DEPTH, 0)
    sel_blk = pl.BlockSpec((H, N_KEYS, PEER_TOK), lambda s: (0, 0, gate(s) // N_CHUNKS))
    return pl.pallas_call(
        functools.partial(_peer_kernel, n_work=n_work),
        grid=(n_work + PIPE_DEPTH,),
        in_specs=[
            pl.BlockSpec((D, PEER_TOK), lambda s: (0, fill(s) // N_CHUNKS)),
            pl.BlockSpec((PEER_EC, D), lambda s: (fill(s) % N_CHUNKS, 0)),
            pl.BlockSpec((D, PEER_EC), lambda s: (0, drain(s) % N_CHUNKS)),
            sel_blk, sel_blk, sel_blk, sel_blk,
        ],
        out_specs=pl.BlockSpec((D, PEER_TOK), lambda s: (0, drain(s) // N_CHUNKS)),
        out_shape=jax.ShapeDtypeStruct((D, N), F32),
        scratch_shapes=[
            pltpu.VMEM((PEER_EC, PEER_TOK), F32),
            pltpu.VMEM((PEER_EC, PEER_TOK), F32),
            pltpu.VMEM((PEER_EC, PEER_TOK), BF16),
            pltpu.VMEM((PEER_EC, PEER_TOK), BF16),
            pltpu.VMEM((D, PEER_TOK), F32),
        ],
        compiler_params=pltpu.CompilerParams(
            dimension_semantics=("arbitrary",),
            vmem_limit_bytes=VMEM_LIMIT_BYTES),
        name="peer_dense",
    )(xT, u_bf, vT_bf, l1w, e1w, r2, e2)


P_X = 0
P_Z = 1024
P_GP = 2048
P_GS = 3072
P_B = 4096
P_C = 4608
P_POOL = 5120
P_DT = 5632
P_COLS = 5760


def pack_w_in(w):
    pad = jnp.zeros((w.shape[0], LANES - 2 * SSD_HEADS), w.dtype)
    return jnp.concatenate([
        w[:, :COL_B], w[:, COL_Z:COL_POOL], w[:, COL_GP:COL_GS], w[:, COL_GS:],
        w[:, COL_B:COL_C], w[:, COL_C:COL_DTF], w[:, COL_POOL:COL_GP],
        w[:, COL_DTF:SCAN_COLS], pad], axis=1).astype(BF16)


def _sigmoid(x):
    return 1.0 / (1.0 + jnp.exp(-x))


CONV_TL = 256
CONV_TC = 512
HALO = 8


def _conv_kernel(cur_ref, prev_ref, next_ref, w_ref, b_ref, o_ref):
    j = pl.program_id(1)
    nj = pl.num_programs(1)
    prev = jnp.where(j == 0, 0.0, prev_ref[0, 0])
    nxt = jnp.where(j == nj - 1, 0.0, next_ref[0, 0])
    ext = jnp.concatenate([prev, cur_ref[0], nxt], axis=0)
    tl = cur_ref.shape[1]
    acc = jnp.zeros((tl, cur_ref.shape[2]), F32) + b_ref[...]
    for k in range(CONV_W):
        off = HALO + k - CONV_W // 2
        acc = acc + w_ref[k:k + 1, :] * ext[off:off + tl]
    o_ref[0] = acc * _sigmoid(acc)


def conv_silu(proj, conv_w8, conv_b):
    b, L, _ = proj.shape
    tl = min(CONV_TL, L)
    n_c = 2048 // CONV_TC
    proj4 = proj.reshape(b, L // HALO, HALO, P_COLS)
    nh = tl // HALO
    last_h = L // HALO - 1
    colmap = lambda c: jnp.where(c < 2, c, c + (P_B // CONV_TC - 2))
    return pl.pallas_call(
        _conv_kernel,
        grid=(b, L // tl, n_c),
        in_specs=[
            pl.BlockSpec((1, tl, CONV_TC), lambda i, j, c: (i, j, colmap(c))),
            pl.BlockSpec((1, 1, HALO, CONV_TC), lambda i, j, c: (i, jnp.maximum(j * nh - 1, 0), 0, colmap(c))),
            pl.BlockSpec((1, 1, HALO, CONV_TC), lambda i, j, c: (i, jnp.minimum((j + 1) * nh, last_h), 0, colmap(c))),
            pl.BlockSpec((8, CONV_TC), lambda i, j, c: (0, c)),
            pl.BlockSpec((1, CONV_TC), lambda i, j, c: (0, c)),
        ],
        out_specs=pl.BlockSpec((1, tl, CONV_TC), lambda i, j, c: (i, j, c)),
        out_shape=jax.ShapeDtypeStruct((b, L, 2048), F32),
        compiler_params=pltpu.CompilerParams(
            dimension_semantics=("parallel", "parallel", "parallel"),
            vmem_limit_bytes=VMEM_LIMIT_BYTES),
        name="conv_silu",
    )(proj, proj4, proj4, conv_w8, conv_b)


Q = SSD_CHUNK
XB_B = 1024
XB_C = 1536


def _split3(v):
    hi = v.astype(BF16)
    r1 = v - hi.astype(F32)
    mid = r1.astype(BF16)
    lo = (r1 - mid.astype(F32)).astype(BF16)
    return hi, mid, lo


SSD_PAIRS = SSD_HEADS // 2
PAIR_W = 2 * SSD_HEAD_DIM


def head_lane_selector():
    k = jnp.arange(2 * LANES)[:, None] % LANES
    col_head = jnp.arange(D_SSD)[None, :] // SSD_HEAD_DIM
    return jnp.stack([k == col_head, k == col_head + SSD_HEADS]).astype(BF16)


def _ssd_decays(dt_raw, a_lanes, bias_lanes, sel, rev):
    z = dt_raw + bias_lanes
    dt = jnp.maximum(z, 0.0) + jnp.log(1.0 + jnp.exp(-jnp.abs(z)))
    v = dt * (-jnp.exp(a_lanes))
    li = lax.broadcasted_iota(jnp.int32, (Q, Q), 0)
    si = lax.broadcasted_iota(jnp.int32, (Q, Q), 1)
    tri = (si >= li) if rev else (si <= li)
    tri_bf = jnp.where(tri, 1.0, 0.0).astype(BF16)
    hi, mid, lo = _split3(v)
    acum = (jnp.dot(tri_bf, hi, preferred_element_type=F32)
            + jnp.dot(tri_bf, mid, preferred_element_type=F32)
            + jnp.dot(tri_bf, lo, preferred_element_type=F32))
    last = 0 if rev else Q - 1
    tot = acum[last:last + 1, :]
    stack = jnp.concatenate([jnp.exp(acum), jnp.exp(tot - acum) * dt,
                             jnp.broadcast_to(jnp.exp(tot), (8, LANES))], axis=0)
    s_hi = stack.astype(BF16)
    s_mid = (stack - s_hi.astype(F32)).astype(BF16)
    spread = jnp.dot(jnp.concatenate([s_hi, s_mid], axis=1), sel, preferred_element_type=F32)
    return dict(tri=tri, acum=acum, acum_t=acum.T, dt_t=dt.T,
                e_in=spread[0:Q], w_end=spread[Q:2 * Q], e_tot=spread[2 * Q:2 * Q + 1],
                lane0=SSD_HEADS if rev else 0)


def _ssd_group(x_ref, g):
    bg = x_ref[0, :, XB_B + g * SSD_STATE:XB_B + (g + 1) * SSD_STATE]
    cg = x_ref[0, :, XB_C + g * SSD_STATE:XB_C + (g + 1) * SSD_STATE].astype(BF16)
    bg_t = bg.T.astype(BF16)
    cb = jnp.dot(cg, bg_t, preferred_element_type=F32)
    return cg, bg_t, cb


def _ssd_pair(d, grp, x_ref, h_scr, y_ref, pr):
    cg, bg_t, cb = grp
    cols = slice(pr * PAIR_W, (pr + 1) * PAIR_W)
    xh = x_ref[0, :, cols]
    xh_bf = xh.astype(BF16)
    halves = []
    for e in range(2):
        ln = d["lane0"] + 2 * pr + e
        seg = d["acum"][:, ln:ln + 1] - d["acum_t"][ln:ln + 1, :]
        decay = jnp.exp(jnp.where(d["tri"], seg, -jnp.inf))
        wgt = (cb * decay * d["dt_t"][ln:ln + 1, :]).astype(BF16)
        halves.append(jnp.dot(wgt, xh_bf, preferred_element_type=F32))
    first = lax.broadcasted_iota(jnp.int32, (Q, PAIR_W), 1) < SSD_HEAD_DIM
    h_t = h_scr[pr]
    y = (jnp.where(first, halves[0], halves[1])
         + d["e_in"][:, cols] * jnp.dot(cg, h_t.astype(BF16), preferred_element_type=F32))
    y_ref[0, :, cols] = y
    xs = (xh * d["w_end"][:, cols]).astype(BF16)
    h_scr[pr] = d["e_tot"][:, cols] * h_t + jnp.dot(bg_t, xs, preferred_element_type=F32)


def _ssd_kernel(xf_ref, dtf_ref, xb_ref, dtb_ref, a_ref, bias_ref, sel_ref, h0f_ref, h0b_ref,
                yf_ref, yb_ref, hf_ref, hb_ref, hf_scr, hb_scr):
    c = pl.program_id(1)

    @pl.when(c == 0)
    def _():
        hf_scr[...] = h0f_ref[0]
        hb_scr[...] = h0b_ref[0]

    df = _ssd_decays(dtf_ref[0], a_ref[...], bias_ref[...], sel_ref[0], False)
    db = _ssd_decays(dtb_ref[0], a_ref[...], bias_ref[...], sel_ref[1], True)
    pairs_per_group = SSD_HPG // 2
    for g in range(SSD_GROUPS):
        gf = _ssd_group(xf_ref, g)
        gb = _ssd_group(xb_ref, g)
        for r in range(pairs_per_group):
            pr = g * pairs_per_group + r
            _ssd_pair(df, gf, xf_ref, hf_scr, yf_ref, pr)
            _ssd_pair(db, gb, xb_ref, hb_scr, yb_ref, pr)

    @pl.when(c == pl.num_programs(1) - 1)
    def _():
        hf_ref[0] = hf_scr[...]
        hb_ref[0] = hb_scr[...]


def ssd_scan(xbc, proj, a_lanes, bias_lanes, sel, h0f, h0b):
    b, L, _ = xbc.shape
    nc = L // Q
    st_shape = (b, SSD_PAIRS, SSD_STATE, PAIR_W)
    st_spec = pl.BlockSpec((1, SSD_PAIRS, SSD_STATE, PAIR_W), lambda i, c: (i, 0, 0, 0))
    dt_blk = P_DT // LANES
    return pl.pallas_call(
        _ssd_kernel,
        grid=(b, nc),
        in_specs=[
            pl.BlockSpec((1, Q, 2048), lambda i, c: (i, c, 0)),
            pl.BlockSpec((1, Q, LANES), lambda i, c: (i, c, dt_blk)),
            pl.BlockSpec((1, Q, 2048), lambda i, c: (i, nc - 1 - c, 0)),
            pl.BlockSpec((1, Q, LANES), lambda i, c: (i, nc - 1 - c, dt_blk)),
            pl.BlockSpec((1, LANES), lambda i, c: (0, 0)),
            pl.BlockSpec((1, LANES), lambda i, c: (0, 0)),
            pl.BlockSpec((2, 2 * LANES, D_SSD), lambda i, c: (0, 0, 0)),
            st_spec, st_spec,
        ],
        out_specs=[
            pl.BlockSpec((1, Q, D_SSD), lambda i, c: (i, c, 0)),
            pl.BlockSpec((1, Q, D_SSD), lambda i, c: (i, nc - 1 - c, 0)),
            st_spec, st_spec,
        ],
        out_shape=[
            jax.ShapeDtypeStruct((b, L, D_SSD), F32),
            jax.ShapeDtypeStruct((b, L, D_SSD), F32),
            jax.ShapeDtypeStruct(st_shape, F32),
            jax.ShapeDtypeStruct(st_shape, F32),
        ],
        scratch_shapes=[
            pltpu.VMEM((SSD_PAIRS, SSD_STATE, PAIR_W), F32),
            pltpu.VMEM((SSD_PAIRS, SSD_STATE, PAIR_W), F32),
        ],
        compiler_params=pltpu.CompilerParams(
            dimension_semantics=("parallel", "arbitrary"),
            vmem_limit_bytes=VMEM_LIMIT_BYTES),
        name="ssd_scan",
    )(xbc, proj, xbc, proj, a_lanes, bias_lanes, sel, h0f, h0b)


POOL_TILE = 512


def _box_offsets(w):
    return w // 2, w - 1 - w // 2


def _pool_group(u_ref, o_ref, pad_scr, r_scr, w, L, row_len, two_d):
    lo, hi = _box_offsets(w)
    tile = min(POOL_TILE, L)
    n_tiles = L // tile
    half = (pad_scr.shape[0] - L) // 2

    if two_d:
        n_rows = L // row_len
        pad_scr[0:half, :] = jnp.zeros((half, LANES), F32)
        pad_scr[half + L:, :] = jnp.zeros((half, LANES), F32)
        pad_scr[half:half + L, :] = u_ref[0]

        def rows_body(i, carry):
            base = pl.multiple_of(i * tile, tile)
            acc = jnp.zeros((tile, LANES), F32)
            for k in range(-lo, hi + 1):
                acc = acc + pad_scr[pl.ds(base + (half + k * row_len), tile), :]
            row = (base + lax.broadcasted_iota(jnp.int32, (tile, LANES), 0)) // row_len
            cnt = jnp.minimum(row + hi, n_rows - 1) - jnp.maximum(row - lo, 0) + 1
            r_scr[pl.ds(base, tile), :] = acc / cnt.astype(F32)
            return carry

        lax.fori_loop(0, n_tiles, rows_body, 0)

    def cols_body(i, carry):
        base = pl.multiple_of(i * tile, tile)
        u = u_ref[0, pl.ds(base, tile), :]
        t = r_scr[pl.ds(base, tile), :] if two_d else u
        col = lax.broadcasted_iota(jnp.int32, (tile, LANES), 0) % row_len
        acc = t
        for k in range(-lo, hi + 1):
            if k == 0:
                continue
            sh = pltpu.roll(t, (-k) % tile, axis=0)
            ok = (col + k >= 0) & (col + k < row_len)
            acc = acc + jnp.where(ok, sh, 0.0)
        cnt = jnp.minimum(col + hi, row_len - 1) - jnp.maximum(col - lo, 0) + 1
        o_ref[0, pl.ds(base, tile), :] = acc / cnt.astype(F32) - u
        return carry

    lax.fori_loop(0, n_tiles, cols_body, 0)


def _pool_kernel(u_ref, o_ref, pad_scr, r_scr, *, L, row_len, two_d):
    g = pl.program_id(1)
    for gi, w in enumerate(POOL_WINDOWS):
        @pl.when(g == gi)
        def _(w=w):
            _pool_group(u_ref, o_ref, pad_scr, r_scr, w, L, row_len, two_d)


def pool_mixed(proj, two_d):
    b, L, _ = proj.shape
    row_len = GRID_W if two_d else L
    pad_rows = 2 * (max(POOL_WINDOWS) // 2) * row_len if two_d else 2 * HALO
    blk0 = P_POOL // LANES
    return pl.pallas_call(
        functools.partial(_pool_kernel, L=L, row_len=row_len, two_d=two_d),
        grid=(b, POOL_GROUPS),
        in_specs=[pl.BlockSpec((1, L, LANES), lambda i, g: (i, 0, blk0 + g))],
        out_specs=pl.BlockSpec((1, L, LANES), lambda i, g: (i, 0, g)),
        out_shape=jax.ShapeDtypeStruct((b, L, POOL_GROUPS * POOL_CH), F32),
        scratch_shapes=[
            pltpu.VMEM((L + pad_rows, LANES), F32),
            pltpu.VMEM((L, LANES), F32),
        ],
        compiler_params=pltpu.CompilerParams(
            dimension_semantics=("parallel", "parallel"),
            vmem_limit_bytes=VMEM_LIMIT_BYTES),
        name="pool_mixed",
    )(proj)


MIX_TM = 256
RMS_GROUP = D_SSD // SSD_GROUPS


def _layer_norm_rows(v, g, b):
    mu = jnp.mean(v, axis=-1, keepdims=True)
    d = v - mu
    var = jnp.mean(d * d, axis=-1, keepdims=True)
    return d * lax.rsqrt(var + LN_EPS) * g + b


def _mixer_kernel(yf_ref, yb_ref, xh_ref, z_ref, gp_ref, gs_ref, mixed_ref, x_ref,
                  dskip_ref, normg_ref, wssd_ref, poolw_ref, pscale_ref, wo_ref,
                  g1_ref, lng_ref, lnb_ref, sc2_ref, sh2_ref, xo_ref, hT_ref):
    y = yf_ref[0] + yb_ref[0] + dskip_ref[...] * xh_ref[0]
    z = z_ref[0]
    yg = y * (z * _sigmoid(z))
    parts = []
    for g in range(SSD_GROUPS):
        blk = yg[:, g * RMS_GROUP:(g + 1) * RMS_GROUP]
        ms = jnp.mean(blk * blk, axis=-1, keepdims=True)
        parts.append(blk * lax.rsqrt(ms + LN_EPS))
    yn = jnp.concatenate(parts, axis=-1) * normg_ref[...]
    y_s = jnp.dot(yn.astype(BF16), wssd_ref[...], preferred_element_type=F32)
    pooled = []
    for g in range(POOL_GROUPS):
        m = mixed_ref[0][:, g * POOL_CH:(g + 1) * POOL_CH].astype(BF16)
        pooled.append(jnp.dot(m, poolw_ref[g], preferred_element_type=F32))
    y_pool = jnp.concatenate(pooled, axis=-1) * pscale_ref[...]
    merged = _sigmoid(gp_ref[0]) * y_pool + _sigmoid(gs_ref[0]) * y_s
    mix = jnp.dot(merged.astype(BF16), wo_ref[...], preferred_element_type=F32)
    xn = _layer_norm_rows(DEEPNORM_ALPHA * x_ref[0] + g1_ref[0] * mix, lng_ref[...], lnb_ref[...])
    xo_ref[0] = xn
    h2 = xn * (1.0 + sc2_ref[0]) + sh2_ref[0]
    hT_ref[...] = h2.T.astype(BF16)


def mixer_epilogue(yf, yb, xbc, proj, mixed, x, dskip, normg, wssd_bf, poolw_bf, pscale, wo_bf,
                   g1, lng, lnb, sc2, sh2):
    b, L, D = x.shape
    tm = min(MIX_TM, L)
    nj = L // tm
    tok = lambda blk: pl.BlockSpec((1, tm, D), lambda i, j, blk=blk: (i, j, blk))
    vec = pl.BlockSpec((1, D), lambda i, j: (0, 0))
    bvec = pl.BlockSpec((1, 1, D), lambda i, j: (i, 0, 0))
    full2 = pl.BlockSpec((D, D), lambda i, j: (0, 0))
    return pl.pallas_call(
        _mixer_kernel,
        grid=(b, nj),
        in_specs=[
            tok(0), tok(0), tok(0),
            tok(P_Z // D), tok(P_GP // D), tok(P_GS // D),
            pl.BlockSpec((1, tm, POOL_GROUPS * POOL_CH), lambda i, j: (i, j, 0)),
            tok(0),
            vec, vec, full2,
            pl.BlockSpec((POOL_GROUPS, POOL_CH, D // POOL_GROUPS), lambda i, j: (0, 0, 0)),
            vec, full2,
            bvec, vec, vec, bvec, bvec,
        ],
        out_specs=[
            pl.BlockSpec((1, tm, D), lambda i, j: (i, j, 0)),
            pl.BlockSpec((D, tm), lambda i, j: (0, i * nj + j)),
        ],
        out_shape=[
            jax.ShapeDtypeStruct((b, L, D), F32),
            jax.ShapeDtypeStruct((D, b * L), BF16),
        ],
        compiler_params=pltpu.CompilerParams(
            dimension_semantics=("parallel", "parallel"),
            vmem_limit_bytes=VMEM_LIMIT_BYTES),
        name="mixer_epilogue",
    )(yf, yb, xbc, proj, proj, proj, mixed, x, dskip, normg, wssd_bf, poolw_bf, pscale, wo_bf,
      g1, lng, lnb, sc2, sh2)


def _peer_out_kernel(x_ref, peT_ref, g2_ref, lng_ref, lnb_ref, o_ref):
    v = DEEPNORM_ALPHA * x_ref[0] + g2_ref[0] * peT_ref[...].T
    o_ref[0] = _layer_norm_rows(v, lng_ref[...], lnb_ref[...])


def peer_epilogue(x, peT, g2, lng, lnb):
    b, L, D = x.shape
    tm = min(MIX_TM, L)
    nj = L // tm
    vec = pl.BlockSpec((1, D), lambda i, j: (0, 0))
    return pl.pallas_call(
        _peer_out_kernel,
        grid=(b, nj),
        in_specs=[
            pl.BlockSpec((1, tm, D), lambda i, j: (i, j, 0)),
            pl.BlockSpec((D, tm), lambda i, j: (0, i * nj + j)),
            pl.BlockSpec((1, 1, D), lambda i, j: (i, 0, 0)),
            vec, vec,
        ],
        out_specs=pl.BlockSpec((1, tm, D), lambda i, j: (i, j, 0)),
        out_shape=jax.ShapeDtypeStruct((b, L, D), F32),
        compiler_params=pltpu.CompilerParams(
            dimension_semantics=("parallel", "parallel"),
            vmem_limit_bytes=VMEM_LIMIT_BYTES),
        name="peer_epilogue",
    )(x, peT, g2, lng, lnb)


def _mixing_sublayer(xin, sc1, sh1, g1, sc2, sh2, h0f, h0b, two_d, w):
    proj = mod_matmul(xin, sc1, sh1, w["w_in"], tm=min(512, xin.shape[1]), tn=1152)
    xbc = conv_silu(proj, w["conv_w8"], w["conv_b"])
    yf, yb, hf, hb = ssd_scan(xbc, proj, w["a_lanes"], w["bias_lanes"], w["sel"], h0f, h0b)
    mixed = pool_mixed(proj, two_d)
    xo, hT = mixer_epilogue(yf, yb, xbc, proj, mixed, xin, w["dskip"], w["normg"], w["w_ssd_out"],
                            w["pool_w"], w["pscale"], w["w_o"], g1, w["ln_g0"], w["ln_b0"], sc2, sh2)
    return xo, hT, hf, hb


def _peer_sublayer(xin, hT, g2, w):
    l1w, e1w, r2, e2 = peer_select(hT, w["wqT"], w["keys"])
    peT = peer_dense(hT, w["u"], w["vT"], l1w, e1w, r2, e2)
    return peer_epilogue(xin, peT, g2, w["ln_g1"], w["ln_b1"])


def kernel(x, c, ctx, c_ctx, w_mod, b_mod, w_in, conv_w, conv_b, a_log, dt_bias, d_skip,
           ssd_norm_g, w_ssd_out, pool_w, pool_scale, w_o, ln_g, ln_b, w_q, sub_keys, u_tab, v_tab):
    B = x.shape[0]
    silu_c = jax.nn.silu(c)
    silu_cc = jax.nn.silu(c_ctx)
    mod_in = jnp.pad(jnp.concatenate([silu_c, silu_cc[None]], axis=0), ((0, 8 - (B + 1)), (0, 0)))
    ones_b = jnp.ones((B, 1, 1), F32)
    pad_lanes = jnp.zeros((LANES - 2 * SSD_HEADS,), F32)
    state0 = jnp.zeros((B, SSD_PAIRS, SSD_STATE, PAIR_W), F32)
    sel = head_lane_selector()
    for l in range(DEPTH):
        last = l == DEPTH - 1
        mod_all = matmul(mod_in, w_mod[l].astype(BF16), tn=1536) + b_mod[l]
        sh1, sc1, g1, sh2, sc2, g2 = jnp.split(mod_all[:B, None, :], 6, axis=-1)
        mc = [m[None, None, :] * ones_b for m in jnp.split(mod_all[B], 6)]
        w = dict(
            w_in=pack_w_in(w_in[l]),
            conv_w8=jnp.pad(conv_w[l], ((0, 8 - CONV_W), (0, 0))),
            conv_b=conv_b[l][None],
            a_lanes=jnp.concatenate([a_log[l, 0], a_log[l, 1], pad_lanes])[None],
            bias_lanes=jnp.concatenate([dt_bias[l, 0], dt_bias[l, 1], pad_lanes])[None],
            sel=sel,
            dskip=jnp.repeat(d_skip[l], SSD_HEAD_DIM)[None],
            normg=ssd_norm_g[l][None],
            w_ssd_out=w_ssd_out[l].astype(BF16),
            pool_w=pool_w[l].astype(BF16),
            pscale=pool_scale[l][None],
            w_o=w_o[l].astype(BF16),
            ln_g0=ln_g[l, 0][None], ln_b0=ln_b[l, 0][None],
            ln_g1=ln_g[l, 1][None], ln_b1=ln_b[l, 1][None],
            wqT=w_q[l].T.astype(BF16),
            keys=sub_keys[l].reshape(2 * PEER_HEADS, N_KEYS, D_HALF).astype(BF16),
            u=u_tab[l].astype(BF16),
            vT=v_tab[l].T.astype(BF16),
        )
        ctx_mix, hT_c, hf_c, hb_c = _mixing_sublayer(ctx, mc[1], mc[0], mc[2], mc[4], mc[3],
                                                     state0, state0, False, w)
        x, hT, _, _ = _mixing_sublayer(x, sc1, sh1, g1, sc2, sh2, hf_c, hb_c, True, w)
        x = _peer_sublayer(x, hT, g2, w)
        if not last:
            ctx = _peer_sublayer(ctx_mix, hT_c, mc[5], w)
    return x
```

```python
import functools

import jax
import jax.numpy as jnp
from jax import lax
from jax.experimental import pallas as pl
from jax.experimental.pallas import tpu as pltpu

D_MODEL = 1024
DEPTH = 4
GRID_W = 64
DEEPNORM_ALPHA = (2.0 * DEPTH) ** 0.25
LN_EPS = 1e-6

SSD_HEAD_DIM = 64
D_SSD = D_MODEL
SSD_HEADS = D_SSD // SSD_HEAD_DIM
SSD_GROUPS = 4
SSD_HPG = SSD_HEADS // SSD_GROUPS
SSD_STATE = 128
SSD_CHUNK = 128
CONV_W = 5

POOL_GROUPS = 4
POOL_CH = D_MODEL // 8
POOL_WINDOWS = (2, 4, 8, 16)

PEER_HEADS = 8
N_KEYS = 128
N_EXPERTS = N_KEYS * N_KEYS
PEER_TOPK = 16
PEER_QDIM = 256
D_HALF = PEER_QDIM // 2

COL_B = D_SSD
COL_C = COL_B + SSD_GROUPS * SSD_STATE
COL_DTF = COL_C + SSD_GROUPS * SSD_STATE
COL_DTB = COL_DTF + SSD_HEADS
SCAN_COLS = COL_DTB + SSD_HEADS
COL_Z = SCAN_COLS
COL_POOL = COL_Z + D_SSD
COL_GP = COL_POOL + POOL_GROUPS * POOL_CH
COL_GS = COL_GP + D_MODEL
W_IN_COLS = COL_GS + D_MODEL

F32 = jnp.float32
BF16 = jnp.bfloat16

LANES = 128
VMEM_LIMIT_BYTES = 56 * 1024 * 1024


def _mod_matmul_kernel(a_ref, sc_ref, sh_ref, w_ref, o_ref):
    a = a_ref[0] * (1.0 + sc_ref[0]) + sh_ref[0]
    o_ref[0] = jnp.dot(a.astype(BF16), w_ref[...], preferred_element_type=F32)


IN_PROJ_TM = 256


def mod_matmul(a, scale, shift, w, tm=512, tn=None):
    b, L, K = a.shape
    N = w.shape[1]
    if tn is None:
        tn = N
    assert L % tm == 0 and N % tn == 0
    return pl.pallas_call(
        _mod_matmul_kernel,
        grid=(b, L // tm, N // tn),
        in_specs=[
            pl.BlockSpec((1, tm, K), lambda i, j, k: (i, j, 0)),
            pl.BlockSpec((1, 1, K), lambda i, j, k: (i, 0, 0)),
            pl.BlockSpec((1, 1, K), lambda i, j, k: (i, 0, 0)),
            pl.BlockSpec((K, tn), lambda i, j, k: (0, k)),
        ],
        out_specs=pl.BlockSpec((1, tm, tn), lambda i, j, k: (i, j, k)),
        out_shape=jax.ShapeDtypeStruct((b, L, N), F32),
        compiler_params=pltpu.CompilerParams(
            dimension_semantics=("parallel", "parallel", "arbitrary"),
            vmem_limit_bytes=VMEM_LIMIT_BYTES),
        name="mod_matmul",
    )(a, scale, shift, w)


def _matmul_kernel(a_ref, w_ref, o_ref):
    o_ref[...] = jnp.dot(a_ref[...].astype(BF16), w_ref[...], preferred_element_type=F32)


def matmul(a, w, tm=512, tn=None):
    M, K = a.shape
    N = w.shape[1]
    if tn is None:
        tn = N
    tm = min(tm, M)
    assert M % tm == 0 and N % tn == 0
    return pl.pallas_call(
        _matmul_kernel,
        grid=(M // tm, N // tn),
        in_specs=[
            pl.BlockSpec((tm, K), lambda i, k: (i, 0)),
            pl.BlockSpec((K, tn), lambda i, k: (0, k)),
        ],
        out_specs=pl.BlockSpec((tm, tn), lambda i, k: (i, k)),
        out_shape=jax.ShapeDtypeStruct((M, N), F32),
        compiler_params=pltpu.CompilerParams(
            dimension_semantics=("parallel", "arbitrary"),
            vmem_limit_bytes=VMEM_LIMIT_BYTES),
        name="matmul",
    )(a, w)


SEL_TOK = 256
NOT_RANKED = 127.0


def _dup_bf16_words(x):
    hi = lax.bitcast_convert_type(x.astype(BF16).astype(F32), jnp.uint32)
    return hi | (hi >> 16)


def _top16(s, vals_ref, want_rank):
    rank = jnp.full(s.shape, NOT_RANKED, F32) if want_rank else None
    for r in range(PEER_TOPK):
        m = jnp.max(s, axis=0, keepdims=True)
        eq = s == m
        if want_rank:
            rank = jnp.where(eq, float(r), rank)
        s = jnp.where(eq, -jnp.inf, s)
        vals_ref[r:r + 1, :] = m
    return rank


def _extract_max_by_index(vals, order):
    m = jnp.max(vals, axis=0, keepdims=True)
    first = jnp.min(jnp.where(vals == m, order, jnp.int32(2 ** 30)), axis=0, keepdims=True)
    return m, order == first


def _top16_by_index(s, vals_ref):
    order = lax.broadcasted_iota(jnp.int32, s.shape, 0)
    rank = jnp.full(s.shape, NOT_RANKED, F32)
    for r in range(PEER_TOPK):
        m, hit = _extract_max_by_index(s, order)
        rank = jnp.where(hit, float(r), rank)
        s = jnp.where(hit, -jnp.inf, s)
        vals_ref[r:r + 1, :] = m
    return rank


CAND_ROWS = 16 + 7 * 8 + 8


def _candidates(v1, v2):
    slabs = [v1[0:1] + v2]
    for a in range(1, 8):
        slabs.append(v1[a:a + 1] + v2[0:8])
    slabs.append(v1[8:16] + v2[0:1])
    return jnp.concatenate(slabs, axis=0)


def _candidate_flat_index(t):
    r = lax.broadcasted_iota(jnp.int32, (CAND_ROWS, t), 0)
    mid = 16 * (1 + (r - 16) // 8) + (r - 16) % 8
    return jnp.where(r < 16, r, jnp.where(r < 72, mid, 16 * (r - 64)))


def _counts_per_first_rank(self):
    counts = [jnp.sum(self[0:16], axis=0, keepdims=True)]
    for a in range(1, 8):
        counts.append(jnp.sum(self[8 + 8 * a:16 + 8 * a], axis=0, keepdims=True))
    for a in range(8, 16):
        counts.append(self[64 + a:65 + a])
    return counts


def _write_selection(h, s1, s2, v1, v2, l1, rank2, self, cand, refs):
    l1w_ref, e1w_ref, r2_ref, e2_ref = refs
    top = v1[0:1] + v2[0:1]
    z = jnp.sum(self * jnp.exp(cand - top), axis=0, keepdims=True)
    e1 = jnp.exp(s1 - v1[0:1])
    e2 = jnp.exp(s2 - v2[0:1]) / z
    l1w_ref[h] = _dup_bf16_words(l1)
    e1w_ref[h] = _dup_bf16_words(e1)
    r2_ref[h] = rank2.astype(BF16)
    e2_ref[h] = e2.astype(BF16)


def _count_rows(mask):
    return jnp.sum(jnp.where(mask, 1.0, 0.0), axis=0, keepdims=True)


def _peer_select_kernel(hT_ref, wqT_ref, keys_ref, l1w_ref, e1w_ref, r2_ref, e2_ref,
                        q_scr, v1_scr, v2_scr, tied_scr):
    q_scr[...] = jnp.dot(wqT_ref[...], hT_ref[...], preferred_element_type=F32).astype(BF16)
    out_refs = (l1w_ref, e1w_ref, r2_ref, e2_ref)
    k = float(PEER_TOPK)

    def scores(h):
        off = pl.multiple_of(h * PEER_QDIM, PEER_QDIM)
        s1 = jnp.dot(keys_ref[2 * h], q_scr[pl.ds(off, D_HALF), :], preferred_element_type=F32)
        s2 = jnp.dot(keys_ref[2 * h + 1], q_scr[pl.ds(off + D_HALF, D_HALF), :],
                     preferred_element_type=F32)
        return s1, s2

    def head(h, carry):
        s1, s2 = scores(h)
        _top16(s1, v1_scr, False)
        rank2 = _top16(s2, v2_scr, True)
        v1 = v1_scr[...]
        v2 = v2_scr[...]
        cand = _candidates(v1, v2)
        rest = cand
        tau = None
        for r in range(PEER_TOPK):
            tau = jnp.max(rest, axis=0, keepdims=True)
            if r < PEER_TOPK - 1:
                rest = jnp.where(rest == tau, -jnp.inf, rest)
        self = jnp.where(cand >= tau, 1.0, 0.0)
        counts = _counts_per_first_rank(self)
        l1 = jnp.zeros(s1.shape, F32)
        for a in range(PEER_TOPK):
            l1 = jnp.where(s1 == v1[a:a + 1], counts[a], l1)
        _write_selection(h, s1, s2, v1, v2, l1, rank2, self, cand, out_refs)
        bad = ((_count_rows(s1 >= v1[15:16]) != k) | (_count_rows(s2 >= v2[15:16]) != k)
               | (jnp.sum(self, axis=0, keepdims=True) != k))
        tied_scr[pl.ds(h, 1), :] = jnp.where(bad, 1.0, 0.0)
        return carry

    lax.fori_loop(0, PEER_HEADS, head, 0, unroll=4)

    def redo_if_tied(h, carry):
        @pl.when(jnp.max(tied_scr[pl.ds(h, 1), :]) > 0.0)
        def _():
            s1, s2 = scores(h)
            rank1 = _top16_by_index(s1, v1_scr)
            rank2 = _top16_by_index(s2, v2_scr)
            v1 = v1_scr[...]
            v2 = v2_scr[...]
            cand = _candidates(v1, v2)
            order = _candidate_flat_index(SEL_TOK)
            rest = cand
            self = jnp.zeros(cand.shape, F32)
            for r in range(PEER_TOPK):
                _, hit = _extract_max_by_index(rest, order)
                self = jnp.where(hit, 1.0, self)
                rest = jnp.where(hit, -jnp.inf, rest)
            counts = _counts_per_first_rank(self)
            l1 = jnp.zeros(s1.shape, F32)
            for a in range(PEER_TOPK):
                l1 = jnp.where(rank1 == float(a), counts[a], l1)
            _write_selection(h, s1, s2, v1, v2, l1, rank2, self, cand, out_refs)
        return carry

    lax.fori_loop(0, PEER_HEADS, redo_if_tied, 0)


def peer_select(hT, wqT_bf, keys_bf):
    D, N = hT.shape
    assert N % SEL_TOK == 0
    H = PEER_HEADS
    blk = pl.BlockSpec((H, N_KEYS, SEL_TOK), lambda t: (0, 0, t))
    return pl.pallas_call(
        _peer_select_kernel,
        grid=(N // SEL_TOK,),
        in_specs=[
            pl.BlockSpec((D, SEL_TOK), lambda t: (0, t)),
            pl.BlockSpec((H * PEER_QDIM, D), lambda t: (0, 0)),
            pl.BlockSpec((2 * H, N_KEYS, D_HALF), lambda t: (0, 0, 0)),
        ],
        out_specs=[blk, blk, blk, blk],
        out_shape=[
            jax.ShapeDtypeStruct((H, N_KEYS, N), jnp.uint32),
            jax.ShapeDtypeStruct((H, N_KEYS, N), jnp.uint32),
            jax.ShapeDtypeStruct((H, N_KEYS, N), BF16),
            jax.ShapeDtypeStruct((H, N_KEYS, N), BF16),
        ],
        scratch_shapes=[
            pltpu.VMEM((H * PEER_QDIM, SEL_TOK), BF16),
            pltpu.VMEM((PEER_TOPK, SEL_TOK), F32),
            pltpu.VMEM((PEER_TOPK, SEL_TOK), F32),
            pltpu.VMEM((H, SEL_TOK), F32),
        ],
        compiler_params=pltpu.CompilerParams(
            dimension_semantics=("parallel",),
            vmem_limit_bytes=VMEM_LIMIT_BYTES),
        name="peer_select",
    )(hT, wqT_bf, keys_bf)


PEER_TOK = 512
PEER_EC = 1024
PEER_LANES = 256
BF16_ROWS = 16

LOG2_E = 1.4426950408889634
GELU_A = -2.0 * 0.7978845608028654 * LOG2_E
GELU_B = GELU_A * 0.044715


def _gelu_tanh(x):
    return x / (1.0 + jnp.exp2(x * (GELU_A + GELU_B * (x * x))))


N_CHUNKS = N_EXPERTS // PEER_EC
PIPE_DEPTH = 2


N_TILE_ROWS = PEER_EC // N_KEYS
N_LANE_GROUPS = PEER_TOK // PEER_LANES
MXU_K = 256
MXU_N = 256


def _peer_gate_rows(chunk, il, lane_groups, s_in, w_out, l1w_ref, e1w_ref, r2_ref, e2_ref):
    n_sub = N_KEYS // BF16_ROWS
    i = chunk * N_TILE_ROWS + il
    for lg in lane_groups:
        lanes = slice(lg * PEER_LANES, (lg + 1) * PEER_LANES)
        gs = [jnp.zeros((BF16_ROWS, PEER_LANES), BF16) for _ in range(n_sub)]
        for h in range(PEER_HEADS):
            l1 = pltpu.bitcast(jnp.broadcast_to(l1w_ref[h, pl.ds(i, 1), lanes], (8, PEER_LANES)), BF16)
            e1 = pltpu.bitcast(jnp.broadcast_to(e1w_ref[h, pl.ds(i, 1), lanes], (8, PEER_LANES)), BF16)
            for sub in range(n_sub):
                rows = slice(sub * BF16_ROWS, (sub + 1) * BF16_ROWS)
                gate = e2_ref[h, rows, lanes] * e1
                gs[sub] = gs[sub] + jnp.where(r2_ref[h, rows, lanes] < l1, gate, jnp.zeros_like(gate))
        for sub in range(n_sub):
            rows = slice(il * N_KEYS + sub * BF16_ROWS, il * N_KEYS + (sub + 1) * BF16_ROWS)
            act = _gelu_tanh(s_in[rows, lanes]).astype(BF16)
            w_out[rows, lanes] = gs[sub] * act


def _peer_kernel(xT_ref, u_ref, vT_ref, l1w_ref, e1w_ref, r2_ref, e2_ref, o_ref,
                 s_a, s_b, w_a, w_b, acc_scr, *, n_work):
    s = pl.program_id(0)

    @pl.when(s == 0)
    def _():
        s_a[...] = jnp.zeros_like(s_a)
        s_b[...] = jnp.zeros_like(s_b)
        w_a[...] = jnp.zeros_like(w_a)
        w_b[...] = jnp.zeros_like(w_b)

    drain = s - PIPE_DEPTH
    drain_chunk = drain % N_CHUNKS

    @pl.when((drain_chunk == 0) | (s == 0))
    def _():
        acc_scr[...] = jnp.zeros_like(acc_scr)

    gate_chunk = jnp.clip(s - 1, 0, n_work - 1) % N_CHUNKS

    def step(s_out, s_in, w_out, w_in):
        d_model = xT_ref.shape[0]
        lane_tiles = range(PEER_TOK // MXU_N)
        pieces1 = [(n, k) for n in lane_tiles for k in range(d_model // MXU_K)]
        pieces2 = [(n, k) for n in lane_tiles for k in range(PEER_EC // MXU_K)]
        assert len(pieces2) == N_TILE_ROWS and N_TILE_ROWS % len(pieces1) == 0
        every = N_TILE_ROWS // len(pieces1)
        gate_refs = (s_in, w_out, l1w_ref, e1w_ref, r2_ref, e2_ref)
        tile = lambda n, k: (slice(n * MXU_N, (n + 1) * MXU_N), slice(k * MXU_K, (k + 1) * MXU_K))
        for il in range(N_TILE_ROWS):
            part1 = None
            if il % every == 0:
                lanes1, ks1 = tile(*pieces1[il // every])
                part1 = jnp.dot(u_ref[:, ks1], xT_ref[ks1, lanes1], preferred_element_type=F32)
            lanes2, ks2 = tile(*pieces2[il])
            part2 = jnp.dot(vT_ref[:, ks2], w_in[ks2, lanes2], preferred_element_type=F32)
            _peer_gate_rows(gate_chunk, il, range(N_LANE_GROUPS), *gate_refs)
            if part1 is not None:
                if pieces1[il // every][1] == 0:
                    s_out[:, lanes1] = part1
                else:
                    s_out[:, lanes1] += part1
            acc_scr[:, lanes2] += part2

    @pl.when(s % 2 == 0)
    def _():
        step(s_a, s_b, w_b, w_a)

    @pl.when(s % 2 == 1)
    def _():
        step(s_b, s_a, w_a, w_b)

    @pl.when((drain >= 0) & (drain_chunk == N_CHUNKS - 1))
    def _():
        o_ref[...] = acc_scr[...]


def peer_dense(xT, u_bf, vT_bf, l1w, e1w, r2, e2):
    D, N = xT.shape
    assert N % PEER_TOK == 0
    H = PEER_HEADS
    n_work = (N // PEER_TOK) * N_CHUNKS
    fill = lambda s: jnp.minimum(s, n_work - 1)
    gate = lambda s: jnp.clip(s - 1, 0, n_work - 1)
    drain = lambda s: jnp.maximum(s - PIPE_DEPTH, 0)
    sel_blk = pl.BlockSpec((H, N_KEYS, PEER_TOK), lambda s: (0, 0, gate(s) // N_CHUNKS))
    return pl.pallas_call(
        functools.partial(_peer_kernel, n_work=n_work),
        grid=(n_work + PIPE_DEPTH,),
        in_specs=[
            pl.BlockSpec((D, PEER_TOK), lambda s: (0, fill(s) // N_CHUNKS)),
            pl.BlockSpec((PEER_EC, D), lambda s: (fill(s) % N_CHUNKS, 0)),
            pl.BlockSpec((D, PEER_EC), lambda s: (0, drain(s) % N_CHUNKS)),
            sel_blk, sel_blk, sel_blk, sel_blk,
        ],
        out_specs=pl.BlockSpec((D, PEER_TOK), lambda s: (0, drain(s) // N_CHUNKS)),
        out_shape=jax.ShapeDtypeStruct((D, N), F32),
        scratch_shapes=[
            pltpu.VMEM((PEER_EC, PEER_TOK), F32),
            pltpu.VMEM((PEER_EC, PEER_TOK), F32),
            pltpu.VMEM((PEER_EC, PEER_TOK), BF16),
            pltpu.VMEM((PEER_EC, PEER_TOK), BF16),
            pltpu.VMEM((D, PEER_TOK), F32),
        ],
        compiler_params=pltpu.CompilerParams(
            dimension_semantics=("arbitrary",),
            vmem_limit_bytes=VMEM_LIMIT_BYTES),
        name="peer_dense",
    )(xT, u_bf, vT_bf, l1w, e1w, r2, e2)


P_X = 0
P_Z = 1024
P_GP = 2048
P_GS = 3072
P_B = 4096
P_C = 4608
P_POOL = 5120
P_DT = 5632
P_COLS = 5760


def pack_w_in(w):
    pad = jnp.zeros((w.shape[0], LANES - 2 * SSD_HEADS), w.dtype)
    return jnp.concatenate([
        w[:, :COL_B], w[:, COL_Z:COL_POOL], w[:, COL_GP:COL_GS], w[:, COL_GS:],
        w[:, COL_B:COL_C], w[:, COL_C:COL_DTF], w[:, COL_POOL:COL_GP],
        w[:, COL_DTF:SCAN_COLS], pad], axis=1).astype(BF16)


def _sigmoid(x):
    return 1.0 / (1.0 + jnp.exp(-x))


CONV_TL = 256
CONV_TC = 512
HALO = 8


def _conv_kernel(cur_ref, prev_ref, next_ref, w_ref, b_ref, o_ref):
    j = pl.program_id(1)
    nj = pl.num_programs(1)
    prev = jnp.where(j == 0, 0.0, prev_ref[0, 0])
    nxt = jnp.where(j == nj - 1, 0.0, next_ref[0, 0])
    ext = jnp.concatenate([prev, cur_ref[0], nxt], axis=0)
    tl = cur_ref.shape[1]
    acc = jnp.zeros((tl, cur_ref.shape[2]), F32) + b_ref[...]
    for k in range(CONV_W):
        off = HALO + k - CONV_W // 2
        acc = acc + w_ref[k:k + 1, :] * ext[off:off + tl]
    o_ref[0] = acc * _sigmoid(acc)


def conv_silu(proj, conv_w8, conv_b):
    b, L, _ = proj.shape
    tl = min(CONV_TL, L)
    n_c = 2048 // CONV_TC
    proj4 = proj.reshape(b, L // HALO, HALO, P_COLS)
    nh = tl // HALO
    last_h = L // HALO - 1
    colmap = lambda c: jnp.where(c < 2, c, c + (P_B // CONV_TC - 2))
    return pl.pallas_call(
        _conv_kernel,
        grid=(b, L // tl, n_c),
        in_specs=[
            pl.BlockSpec((1, tl, CONV_TC), lambda i, j, c: (i, j, colmap(c))),
            pl.BlockSpec((1, 1, HALO, CONV_TC), lambda i, j, c: (i, jnp.maximum(j * nh - 1, 0), 0, colmap(c))),
            pl.BlockSpec((1, 1, HALO, CONV_TC), lambda i, j, c: (i, jnp.minimum((j + 1) * nh, last_h), 0, colmap(c))),
            pl.BlockSpec((8, CONV_TC), lambda i, j, c: (0, c)),
            pl.BlockSpec((1, CONV_TC), lambda i, j, c: (0, c)),
        ],
        out_specs=pl.BlockSpec((1, tl, CONV_TC), lambda i, j, c: (i, j, c)),
        out_shape=jax.ShapeDtypeStruct((b, L, 2048), F32),
        compiler_params=pltpu.CompilerParams(
            dimension_semantics=("parallel", "parallel", "parallel"),
            vmem_limit_bytes=VMEM_LIMIT_BYTES),
        name="conv_silu",
    )(proj, proj4, proj4, conv_w8, conv_b)


Q = SSD_CHUNK
XB_B = 1024
XB_C = 1536


def _split3(v):
    hi = v.astype(BF16)
    r1 = v - hi.astype(F32)
    mid = r1.astype(BF16)
    lo = (r1 - mid.astype(F32)).astype(BF16)
    return hi, mid, lo


SSD_PAIRS = SSD_HEADS // 2
PAIR_W = 2 * SSD_HEAD_DIM


def head_lane_selector():
    k = jnp.arange(2 * LANES)[:, None] % LANES
    col_head = jnp.arange(D_SSD)[None, :] // SSD_HEAD_DIM
    return jnp.stack([k == col_head, k == col_head + SSD_HEADS]).astype(BF16)


def _ssd_decays(dt_raw, a_lanes, bias_lanes, sel, rev):
    z = dt_raw + bias_lanes
    dt = jnp.maximum(z, 0.0) + jnp.log(1.0 + jnp.exp(-jnp.abs(z)))
    v = dt * (-jnp.exp(a_lanes))
    li = lax.broadcasted_iota(jnp.int32, (Q, Q), 0)
    si = lax.broadcasted_iota(jnp.int32, (Q, Q), 1)
    tri = (si >= li) if rev else (si <= li)
    tri_bf = jnp.where(tri, 1.0, 0.0).astype(BF16)
    hi, mid, lo = _split3(v)
    acum = (jnp.dot(tri_bf, hi, preferred_element_type=F32)
            + jnp.dot(tri_bf, mid, preferred_element_type=F32)
            + jnp.dot(tri_bf, lo, preferred_element_type=F32))
    last = 0 if rev else Q - 1
    tot = acum[last:last + 1, :]
    stack = jnp.concatenate([jnp.exp(acum), jnp.exp(tot - acum) * dt,
                             jnp.broadcast_to(jnp.exp(tot), (8, LANES))], axis=0)
    s_hi = stack.astype(BF16)
    s_mid = (stack - s_hi.astype(F32)).astype(BF16)
    spread = jnp.dot(jnp.concatenate([s_hi, s_mid], axis=1), sel, preferred_element_type=F32)
    return dict(tri=tri, acum=acum, acum_t=acum.T, dt_t=dt.T,
                e_in=spread[0:Q], w_end=spread[Q:2 * Q], e_tot=spread[2 * Q:2 * Q + 1],
                lane0=SSD_HEADS if rev else 0)


def _ssd_group(x_ref, g):
    bg = x_ref[0, :, XB_B + g * SSD_STATE:XB_B + (g + 1) * SSD_STATE]
    cg = x_ref[0, :, XB_C + g * SSD_STATE:XB_C + (g + 1) * SSD_STATE].astype(BF16)
    bg_t = bg.T.astype(BF16)
    cb = jnp.dot(cg, bg_t, preferred_element_type=F32)
    return cg, bg_t, cb


def _ssd_pair(d, grp, x_ref, h_scr, y_ref, pr):
    cg, bg_t, cb = grp
    cols = slice(pr * PAIR_W, (pr + 1) * PAIR_W)
    xh = x_ref[0, :, cols]
    xh_bf = xh.astype(BF16)
    halves = []
    for e in range(2):
        ln = d["lane0"] + 2 * pr + e
        seg = d["acum"][:, ln:ln + 1] - d["acum_t"][ln:ln + 1, :]
        decay = jnp.exp(jnp.where(d["tri"], seg, -jnp.inf))
        wgt = (cb * decay * d["dt_t"][ln:ln + 1, :]).astype(BF16)
        halves.append(jnp.dot(wgt, xh_bf, preferred_element_type=F32))
    first = lax.broadcasted_iota(jnp.int32, (Q, PAIR_W), 1) < SSD_HEAD_DIM
    h_t = h_scr[pr]
    y = (jnp.where(first, halves[0], halves[1])
         + d["e_in"][:, cols] * jnp.dot(cg, h_t.astype(BF16), preferred_element_type=F32))
    y_ref[0, :, cols] = y
    xs = (xh * d["w_end"][:, cols]).astype(BF16)
    h_scr[pr] = d["e_tot"][:, cols] * h_t + jnp.dot(bg_t, xs, preferred_element_type=F32)


def _ssd_kernel(xf_ref, dtf_ref, xb_ref, dtb_ref, a_ref, bias_ref, sel_ref, h0f_ref, h0b_ref,
                yf_ref, yb_ref, hf_ref, hb_ref, hf_scr, hb_scr):
    c = pl.program_id(1)

    @pl.when(c == 0)
    def _():
        hf_scr[...] = h0f_ref[0]
        hb_scr[...] = h0b_ref[0]

    df = _ssd_decays(dtf_ref[0], a_ref[...], bias_ref[...], sel_ref[0], False)
    db = _ssd_decays(dtb_ref[0], a_ref[...], bias_ref[...], sel_ref[1], True)
    pairs_per_group = SSD_HPG // 2
    for g in range(SSD_GROUPS):
        gf = _ssd_group(xf_ref, g)
        gb = _ssd_group(xb_ref, g)
        for r in range(pairs_per_group):
            pr = g * pairs_per_group + r
            _ssd_pair(df, gf, xf_ref, hf_scr, yf_ref, pr)
            _ssd_pair(db, gb, xb_ref, hb_scr, yb_ref, pr)

    @pl.when(c == pl.num_programs(1) - 1)
    def _():
        hf_ref[0] = hf_scr[...]
        hb_ref[0] = hb_scr[...]


def ssd_scan(xbc, proj, a_lanes, bias_lanes, sel, h0f, h0b):
    b, L, _ = xbc.shape
    nc = L // Q
    st_shape = (b, SSD_PAIRS, SSD_STATE, PAIR_W)
    st_spec = pl.BlockSpec((1, SSD_PAIRS, SSD_STATE, PAIR_W), lambda i, c: (i, 0, 0, 0))
    dt_blk = P_DT // LANES
    return pl.pallas_call(
        _ssd_kernel,
        grid=(b, nc),
        in_specs=[
            pl.BlockSpec((1, Q, 2048), lambda i, c: (i, c, 0)),
            pl.BlockSpec((1, Q, LANES), lambda i, c: (i, c, dt_blk)),
            pl.BlockSpec((1, Q, 2048), lambda i, c: (i, nc - 1 - c, 0)),
            pl.BlockSpec((1, Q, LANES), lambda i, c: (i, nc - 1 - c, dt_blk)),
            pl.BlockSpec((1, LANES), lambda i, c: (0, 0)),
            pl.BlockSpec((1, LANES), lambda i, c: (0, 0)),
            pl.BlockSpec((2, 2 * LANES, D_SSD), lambda i, c: (0, 0, 0)),
            st_spec, st_spec,
        ],
        out_specs=[
            pl.BlockSpec((1, Q, D_SSD), lambda i, c: (i, c, 0)),
            pl.BlockSpec((1, Q, D_SSD), lambda i, c: (i, nc - 1 - c, 0)),
            st_spec, st_spec,
        ],
        out_shape=[
            jax.ShapeDtypeStruct((b, L, D_SSD), F32),
            jax.ShapeDtypeStruct((b, L, D_SSD), F32),
            jax.ShapeDtypeStruct(st_shape, F32),
            jax.ShapeDtypeStruct(st_shape, F32),
        ],
        scratch_shapes=[
            pltpu.VMEM((SSD_PAIRS, SSD_STATE, PAIR_W), F32),
            pltpu.VMEM((SSD_PAIRS, SSD_STATE, PAIR_W), F32),
        ],
        compiler_params=pltpu.CompilerParams(
            dimension_semantics=("parallel", "arbitrary"),
            vmem_limit_bytes=VMEM_LIMIT_BYTES),
        name="ssd_scan",
    )(xbc, proj, xbc, proj, a_lanes, bias_lanes, sel, h0f, h0b)


POOL_TILE = 512


def _box_offsets(w):
    return w // 2, w - 1 - w // 2


def _pool_group(u_ref, o_ref, pad_scr, r_scr, w, L, row_len, two_d):
    lo, hi = _box_offsets(w)
    tile = min(POOL_TILE, L)
    n_tiles = L // tile
    half = (pad_scr.shape[0] - L) // 2

    if two_d:
        n_rows = L // row_len
        pad_scr[0:half, :] = jnp.zeros((half, LANES), F32)
        pad_scr[half + L:, :] = jnp.zeros((half, LANES), F32)
        pad_scr[half:half + L, :] = u_ref[0]

        def rows_body(i, carry):
            base = pl.multiple_of(i * tile, tile)
            acc = jnp.zeros((tile, LANES), F32)
            for k in range(-lo, hi + 1):
                acc = acc + pad_scr[pl.ds(base + (half + k * row_len), tile), :]
            row = (base + lax.broadcasted_iota(jnp.int32, (tile, LANES), 0)) // row_len
            cnt = jnp.minimum(row + hi, n_rows - 1) - jnp.maximum(row - lo, 0) + 1
            r_scr[pl.ds(base, tile), :] = acc / cnt.astype(F32)
            return carry

        lax.fori_loop(0, n_tiles, rows_body, 0)

    def cols_body(i, carry):
        base = pl.multiple_of(i * tile, tile)
        u = u_ref[0, pl.ds(base, tile), :]
        t = r_scr[pl.ds(base, tile), :] if two_d else u
        col = lax.broadcasted_iota(jnp.int32, (tile, LANES), 0) % row_len
        acc = t
        for k in range(-lo, hi + 1):
            if k == 0:
                continue
            sh = pltpu.roll(t, (-k) % tile, axis=0)
            ok = (col + k >= 0) & (col + k < row_len)
            acc = acc + jnp.where(ok, sh, 0.0)
        cnt = jnp.minimum(col + hi, row_len - 1) - jnp.maximum(col - lo, 0) + 1
        o_ref[0, pl.ds(base, tile), :] = acc / cnt.astype(F32) - u
        return carry

    lax.fori_loop(0, n_tiles, cols_body, 0)


def _pool_kernel(u_ref, o_ref, pad_scr, r_scr, *, L, row_len, two_d):
    g = pl.program_id(1)
    for gi, w in enumerate(POOL_WINDOWS):
        @pl.when(g == gi)
        def _(w=w):
            _pool_group(u_ref, o_ref, pad_scr, r_scr, w, L, row_len, two_d)


def pool_mixed(proj, two_d):
    b, L, _ = proj.shape
    row_len = GRID_W if two_d else L
    pad_rows = 2 * (max(POOL_WINDOWS) // 2) * row_len if two_d else 2 * HALO
    blk0 = P_POOL // LANES
    return pl.pallas_call(
        functools.partial(_pool_kernel, L=L, row_len=row_len, two_d=two_d),
        grid=(b, POOL_GROUPS),
        in_specs=[pl.BlockSpec((1, L, LANES), lambda i, g: (i, 0, blk0 + g))],
        out_specs=pl.BlockSpec((1, L, LANES), lambda i, g: (i, 0, g)),
        out_shape=jax.ShapeDtypeStruct((b, L, POOL_GROUPS * POOL_CH), F32),
        scratch_shapes=[
            pltpu.VMEM((L + pad_rows, LANES), F32),
            pltpu.VMEM((L, LANES), F32),
        ],
        compiler_params=pltpu.CompilerParams(
            dimension_semantics=("parallel", "parallel"),
            vmem_limit_bytes=VMEM_LIMIT_BYTES),
        name="pool_mixed",
    )(proj)


MIX_TM = 256
RMS_GROUP = D_SSD // SSD_GROUPS


def _layer_norm_rows(v, g, b):
    mu = jnp.mean(v, axis=-1, keepdims=True)
    d = v - mu
    var = jnp.mean(d * d, axis=-1, keepdims=True)
    return d * lax.rsqrt(var + LN_EPS) * g + b


def _mixer_kernel(yf_ref, yb_ref, xh_ref, z_ref, gp_ref, gs_ref, mixed_ref, x_ref,
                  dskip_ref, normg_ref, wssd_ref, poolw_ref, pscale_ref, wo_ref,
                  g1_ref, lng_ref, lnb_ref, sc2_ref, sh2_ref, xo_ref, hT_ref):
    y = yf_ref[0] + yb_ref[0] + dskip_ref[...] * xh_ref[0]
    z = z_ref[0]
    yg = y * (z * _sigmoid(z))
    parts = []
    for g in range(SSD_GROUPS):
        blk = yg[:, g * RMS_GROUP:(g + 1) * RMS_GROUP]
        ms = jnp.mean(blk * blk, axis=-1, keepdims=True)
        parts.append(blk * lax.rsqrt(ms + LN_EPS))
    yn = jnp.concatenate(parts, axis=-1) * normg_ref[...]
    y_s = jnp.dot(yn.astype(BF16), wssd_ref[...], preferred_element_type=F32)
    pooled = []
    for g in range(POOL_GROUPS):
        m = mixed_ref[0][:, g * POOL_CH:(g + 1) * POOL_CH].astype(BF16)
        pooled.append(jnp.dot(m, poolw_ref[g], preferred_element_type=F32))
    y_pool = jnp.concatenate(pooled, axis=-1) * pscale_ref[...]
    merged = _sigmoid(gp_ref[0]) * y_pool + _sigmoid(gs_ref[0]) * y_s
    mix = jnp.dot(merged.astype(BF16), wo_ref[...], preferred_element_type=F32)
    xn = _layer_norm_rows(DEEPNORM_ALPHA * x_ref[0] + g1_ref[0] * mix, lng_ref[...], lnb_ref[...])
    xo_ref[0] = xn
    h2 = xn * (1.0 + sc2_ref[0]) + sh2_ref[0]
    hT_ref[...] = h2.T.astype(BF16)


def mixer_epilogue(yf, yb, xbc, proj, mixed, x, dskip, normg, wssd_bf, poolw_bf, pscale, wo_bf,
                   g1, lng, lnb, sc2, sh2):
    b, L, D = x.shape
    tm = min(MIX_TM, L)
    nj = L // tm
    tok = lambda blk: pl.BlockSpec((1, tm, D), lambda i, j, blk=blk: (i, j, blk))
    vec = pl.BlockSpec((1, D), lambda i, j: (0, 0))
    bvec = pl.BlockSpec((1, 1, D), lambda i, j: (i, 0, 0))
    full2 = pl.BlockSpec((D, D), lambda i, j: (0, 0))
    return pl.pallas_call(
        _mixer_kernel,
        grid=(b, nj),
        in_specs=[
            tok(0), tok(0), tok(0),
            tok(P_Z // D), tok(P_GP // D), tok(P_GS // D),
            pl.BlockSpec((1, tm, POOL_GROUPS * POOL_CH), lambda i, j: (i, j, 0)),
            tok(0),
            vec, vec, full2,
            pl.BlockSpec((POOL_GROUPS, POOL_CH, D // POOL_GROUPS), lambda i, j: (0, 0, 0)),
            vec, full2,
            bvec, vec, vec, bvec, bvec,
        ],
        out_specs=[
            pl.BlockSpec((1, tm, D), lambda i, j: (i, j, 0)),
            pl.BlockSpec((D, tm), lambda i, j: (0, i * nj + j)),
        ],
        out_shape=[
            jax.ShapeDtypeStruct((b, L, D), F32),
            jax.ShapeDtypeStruct((D, b * L), BF16),
        ],
        compiler_params=pltpu.CompilerParams(
            dimension_semantics=("parallel", "parallel"),
            vmem_limit_bytes=VMEM_LIMIT_BYTES),
        name="mixer_epilogue",
    )(yf, yb, xbc, proj, proj, proj, mixed, x, dskip, normg, wssd_bf, poolw_bf, pscale, wo_bf,
      g1, lng, lnb, sc2, sh2)


def _peer_out_kernel(x_ref, peT_ref, g2_ref, lng_ref, lnb_ref, o_ref):
    v = DEEPNORM_ALPHA * x_ref[0] + g2_ref[0] * peT_ref[...].T
    o_ref[0] = _layer_norm_rows(v, lng_ref[...], lnb_ref[...])


def peer_epilogue(x, peT, g2, lng, lnb):
    b, L, D = x.shape
    tm = min(MIX_TM, L)
    nj = L // tm
    vec = pl.BlockSpec((1, D), lambda i, j: (0, 0))
    return pl.pallas_call(
        _peer_out_kernel,
        grid=(b, nj),
        in_specs=[
            pl.BlockSpec((1, tm, D), lambda i, j: (i, j, 0)),
            pl.BlockSpec((D, tm), lambda i, j: (0, i * nj + j)),
            pl.BlockSpec((1, 1, D), lambda i, j: (i, 0, 0)),
            vec, vec,
        ],
        out_specs=pl.BlockSpec((1, tm, D), lambda i, j: (i, j, 0)),
        out_shape=jax.ShapeDtypeStruct((b, L, D), F32),
        compiler_params=pltpu.CompilerParams(
            dimension_semantics=("parallel", "parallel"),
            vmem_limit_bytes=VMEM_LIMIT_BYTES),
        name="peer_epilogue",
    )(x, peT, g2, lng, lnb)


def _mixing_sublayer(xin, sc1, sh1, g1, sc2, sh2, h0f, h0b, two_d, w):
    proj = mod_matmul(xin, sc1, sh1, w["w_in"], tm=min(IN_PROJ_TM, xin.shape[1]))
    xbc = conv_silu(proj, w["conv_w8"], w["conv_b"])
    yf, yb, hf, hb = ssd_scan(xbc, proj, w["a_lanes"], w["bias_lanes"], w["sel"], h0f, h0b)
    mixed = pool_mixed(proj, two_d)
    xo, hT = mixer_epilogue(yf, yb, xbc, proj, mixed, xin, w["dskip"], w["normg"], w["w_ssd_out"],
                            w["pool_w"], w["pscale"], w["w_o"], g1, w["ln_g0"], w["ln_b0"], sc2, sh2)
    return xo, hT, hf, hb


def _peer_sublayer(xin, hT, g2, w):
    l1w, e1w, r2, e2 = peer_select(hT, w["wqT"], w["keys"])
    peT = peer_dense(hT, w["u"], w["vT"], l1w, e1w, r2, e2)
    return peer_epilogue(xin, peT, g2, w["ln_g1"], w["ln_b1"])


def kernel(x, c, ctx, c_ctx, w_mod, b_mod, w_in, conv_w, conv_b, a_log, dt_bias, d_skip,
           ssd_norm_g, w_ssd_out, pool_w, pool_scale, w_o, ln_g, ln_b, w_q, sub_keys, u_tab, v_tab):
    B = x.shape[0]
    silu_c = jax.nn.silu(c)
    silu_cc = jax.nn.silu(c_ctx)
    mod_in = jnp.pad(jnp.concatenate([silu_c, silu_cc[None]], axis=0), ((0, 8 - (B + 1)), (0, 0)))
    ones_b = jnp.ones((B, 1, 1), F32)
    pad_lanes = jnp.zeros((LANES - 2 * SSD_HEADS,), F32)
    state0 = jnp.zeros((B, SSD_PAIRS, SSD_STATE, PAIR_W), F32)
    sel = head_lane_selector()
    for l in range(DEPTH):
        last = l == DEPTH - 1
        mod_all = matmul(mod_in, w_mod[l].astype(BF16), tn=1536) + b_mod[l]
        sh1, sc1, g1, sh2, sc2, g2 = jnp.split(mod_all[:B, None, :], 6, axis=-1)
        mc = [m[None, None, :] * ones_b for m in jnp.split(mod_all[B], 6)]
        w = dict(
            w_in=pack_w_in(w_in[l]),
            conv_w8=jnp.pad(conv_w[l], ((0, 8 - CONV_W), (0, 0))),
            conv_b=conv_b[l][None],
            a_lanes=jnp.concatenate([a_log[l, 0], a_log[l, 1], pad_lanes])[None],
            bias_lanes=jnp.concatenate([dt_bias[l, 0], dt_bias[l, 1], pad_lanes])[None],
            sel=sel,
            dskip=jnp.repeat(d_skip[l], SSD_HEAD_DIM)[None],
            normg=ssd_norm_g[l][None],
            w_ssd_out=w_ssd_out[l].astype(BF16),
            pool_w=pool_w[l].astype(BF16),
            pscale=pool_scale[l][None],
            w_o=w_o[l].astype(BF16),
            ln_g0=ln_g[l, 0][None], ln_b0=ln_b[l, 0][None],
            ln_g1=ln_g[l, 1][None], ln_b1=ln_b[l, 1][None],
            wqT=w_q[l].T.astype(BF16),
            keys=sub_keys[l].reshape(2 * PEER_HEADS, N_KEYS, D_HALF).astype(BF16),
            u=u_tab[l].astype(BF16),
            vT=v_tab[l].T.astype(BF16),
        )
        ctx_mix, hT_c, hf_c, hb_c = _mixing_sublayer(ctx, mc[1], mc[0], mc[2], mc[4], mc[3],
                                                     state0, state0, False, w)
        x, hT, _, _ = _mixing_sublayer(x, sc1, sh1, g1, sc2, sh2, hf_c, hb_c, True, w)
        x = _peer_sublayer(x, hT, g2, w)
        if not last:
            ctx = _peer_sublayer(ctx_mix, hT_c, mc[5], w)
    return x
```

```python
import functools

import jax
import jax.numpy as jnp
from jax import lax
from jax.experimental import pallas as pl
from jax.experimental.pallas import tpu as pltpu

D_MODEL = 1024
DEPTH = 4
GRID_W = 64
DEEPNORM_ALPHA = (2.0 * DEPTH) ** 0.25
LN_EPS = 1e-6

SSD_HEAD_DIM = 64
D_SSD = D_MODEL
SSD_HEADS = D_SSD // SSD_HEAD_DIM
SSD_GROUPS = 4
SSD_HPG = SSD_HEADS // SSD_GROUPS
SSD_STATE = 128
SSD_CHUNK = 128
CONV_W = 5

POOL_GROUPS = 4
POOL_CH = D_MODEL // 8
POOL_WINDOWS = (2, 4, 8, 16)

PEER_HEADS = 8
N_KEYS = 128
N_EXPERTS = N_KEYS * N_KEYS
PEER_TOPK = 16
PEER_QDIM = 256
D_HALF = PEER_QDIM // 2

COL_B = D_SSD
COL_C = COL_B + SSD_GROUPS * SSD_STATE
COL_DTF = COL_C + SSD_GROUPS * SSD_STATE
COL_DTB = COL_DTF + SSD_HEADS
SCAN_COLS = COL_DTB + SSD_HEADS
COL_Z = SCAN_COLS
COL_POOL = COL_Z + D_SSD
COL_GP = COL_POOL + POOL_GROUPS * POOL_CH
COL_GS = COL_GP + D_MODEL
W_IN_COLS = COL_GS + D_MODEL

F32 = jnp.float32
BF16 = jnp.bfloat16

LANES = 128
VMEM_LIMIT_BYTES = 56 * 1024 * 1024


def _mod_matmul_kernel(a_ref, sc_ref, sh_ref, w_ref, o_ref):
    a = a_ref[0] * (1.0 + sc_ref[0]) + sh_ref[0]
    o_ref[0] = jnp.dot(a.astype(BF16), w_ref[...], preferred_element_type=F32)


IN_PROJ_TM = 256


def mod_matmul(a, scale, shift, w, tm=512, tn=None):
    b, L, K = a.shape
    N = w.shape[1]
    if tn is None:
        tn = N
    assert L % tm == 0 and N % tn == 0
    return pl.pallas_call(
        _mod_matmul_kernel,
        grid=(b, L // tm, N // tn),
        in_specs=[
            pl.BlockSpec((1, tm, K), lambda i, j, k: (i, j, 0)),
            pl.BlockSpec((1, 1, K), lambda i, j, k: (i, 0, 0)),
            pl.BlockSpec((1, 1, K), lambda i, j, k: (i, 0, 0)),
            pl.BlockSpec((K, tn), lambda i, j, k: (0, k)),
        ],
        out_specs=pl.BlockSpec((1, tm, tn), lambda i, j, k: (i, j, k)),
        out_shape=jax.ShapeDtypeStruct((b, L, N), F32),
        compiler_params=pltpu.CompilerParams(
            dimension_semantics=("parallel", "parallel", "arbitrary"),
            vmem_limit_bytes=VMEM_LIMIT_BYTES),
        name="mod_matmul",
    )(a, scale, shift, w)


def _matmul_kernel(a_ref, w_ref, o_ref):
    o_ref[...] = jnp.dot(a_ref[...].astype(BF16), w_ref[...], preferred_element_type=F32)


def matmul(a, w, tm=512, tn=None):
    M, K = a.shape
    N = w.shape[1]
    if tn is None:
        tn = N
    tm = min(tm, M)
    assert M % tm == 0 and N % tn == 0
    return pl.pallas_call(
        _matmul_kernel,
        grid=(M // tm, N // tn),
        in_specs=[
            pl.BlockSpec((tm, K), lambda i, k: (i, 0)),
            pl.BlockSpec((K, tn), lambda i, k: (0, k)),
        ],
        out_specs=pl.BlockSpec((tm, tn), lambda i, k: (i, k)),
        out_shape=jax.ShapeDtypeStruct((M, N), F32),
        compiler_params=pltpu.CompilerParams(
            dimension_semantics=("parallel", "arbitrary"),
            vmem_limit_bytes=VMEM_LIMIT_BYTES),
        name="matmul",
    )(a, w)


SEL_TOK = 256
NOT_RANKED = 127.0


def _dup_bf16_words(x):
    hi = lax.bitcast_convert_type(x.astype(BF16).astype(F32), jnp.uint32)
    return hi | (hi >> 16)


def _top16(s, vals_ref, want_rank):
    rank = jnp.full(s.shape, NOT_RANKED, F32) if want_rank else None
    for r in range(PEER_TOPK):
        m = jnp.max(s, axis=0, keepdims=True)
        eq = s == m
        if want_rank:
            rank = jnp.where(eq, float(r), rank)
        s = jnp.where(eq, -jnp.inf, s)
        vals_ref[r:r + 1, :] = m
    return rank


def _extract_max_by_index(vals, order):
    m = jnp.max(vals, axis=0, keepdims=True)
    first = jnp.min(jnp.where(vals == m, order, jnp.int32(2 ** 30)), axis=0, keepdims=True)
    return m, order == first


def _top16_by_index(s, vals_ref):
    order = lax.broadcasted_iota(jnp.int32, s.shape, 0)
    rank = jnp.full(s.shape, NOT_RANKED, F32)
    for r in range(PEER_TOPK):
        m, hit = _extract_max_by_index(s, order)
        rank = jnp.where(hit, float(r), rank)
        s = jnp.where(hit, -jnp.inf, s)
        vals_ref[r:r + 1, :] = m
    return rank


CAND_ROWS = 16 + 7 * 8 + 8


def _candidates(v1, v2):
    slabs = [v1[0:1] + v2]
    for a in range(1, 8):
        slabs.append(v1[a:a + 1] + v2[0:8])
    slabs.append(v1[8:16] + v2[0:1])
    return jnp.concatenate(slabs, axis=0)


def _candidate_flat_index(t):
    r = lax.broadcasted_iota(jnp.int32, (CAND_ROWS, t), 0)
    mid = 16 * (1 + (r - 16) // 8) + (r - 16) % 8
    return jnp.where(r < 16, r, jnp.where(r < 72, mid, 16 * (r - 64)))


def _counts_per_first_rank(self):
    counts = [jnp.sum(self[0:16], axis=0, keepdims=True)]
    for a in range(1, 8):
        counts.append(jnp.sum(self[8 + 8 * a:16 + 8 * a], axis=0, keepdims=True))
    for a in range(8, 16):
        counts.append(self[64 + a:65 + a])
    return counts


def _write_selection(h, s1, s2, v1, v2, l1, rank2, self, cand, refs):
    l1w_ref, e1w_ref, r2_ref, e2_ref = refs
    top = v1[0:1] + v2[0:1]
    z = jnp.sum(self * jnp.exp(cand - top), axis=0, keepdims=True)
    e1 = jnp.exp(s1 - v1[0:1])
    e2 = jnp.exp(s2 - v2[0:1]) / z
    l1w_ref[h] = _dup_bf16_words(l1)
    e1w_ref[h] = _dup_bf16_words(e1)
    r2_ref[h] = rank2.astype(BF16)
    e2_ref[h] = e2.astype(BF16)


def _count_rows(mask):
    return jnp.sum(jnp.where(mask, 1.0, 0.0), axis=0, keepdims=True)


def _peer_select_kernel(hT_ref, wqT_ref, keys_ref, l1w_ref, e1w_ref, r2_ref, e2_ref,
                        q_scr, v1_scr, v2_scr, tied_scr):
    q_scr[...] = jnp.dot(wqT_ref[...], hT_ref[...], preferred_element_type=F32).astype(BF16)
    out_refs = (l1w_ref, e1w_ref, r2_ref, e2_ref)
    k = float(PEER_TOPK)

    def scores(h):
        off = pl.multiple_of(h * PEER_QDIM, PEER_QDIM)
        s1 = jnp.dot(keys_ref[2 * h], q_scr[pl.ds(off, D_HALF), :], preferred_element_type=F32)
        s2 = jnp.dot(keys_ref[2 * h + 1], q_scr[pl.ds(off + D_HALF, D_HALF), :],
                     preferred_element_type=F32)
        return s1, s2

    def head(h, carry):
        s1, s2 = scores(h)
        _top16(s1, v1_scr, False)
        rank2 = _top16(s2, v2_scr, True)
        v1 = v1_scr[...]
        v2 = v2_scr[...]
        cand = _candidates(v1, v2)
        rest = cand
        tau = None
        for r in range(PEER_TOPK):
            tau = jnp.max(rest, axis=0, keepdims=True)
            if r < PEER_TOPK - 1:
                rest = jnp.where(rest == tau, -jnp.inf, rest)
        self = jnp.where(cand >= tau, 1.0, 0.0)
        counts = _counts_per_first_rank(self)
        l1 = jnp.zeros(s1.shape, F32)
        for a in range(PEER_TOPK):
            l1 = jnp.where(s1 == v1[a:a + 1], counts[a], l1)
        _write_selection(h, s1, s2, v1, v2, l1, rank2, self, cand, out_refs)
        bad = ((_count_rows(s1 >= v1[15:16]) != k) | (_count_rows(s2 >= v2[15:16]) != k)
               | (jnp.sum(self, axis=0, keepdims=True) != k))
        tied_scr[pl.ds(h, 1), :] = jnp.where(bad, 1.0, 0.0)
        return carry

    lax.fori_loop(0, PEER_HEADS, head, 0, unroll=4)

    def redo_if_tied(h, carry):
        @pl.when(jnp.max(tied_scr[pl.ds(h, 1), :]) > 0.0)
        def _():
            s1, s2 = scores(h)
            rank1 = _top16_by_index(s1, v1_scr)
            rank2 = _top16_by_index(s2, v2_scr)
            v1 = v1_scr[...]
            v2 = v2_scr[...]
            cand = _candidates(v1, v2)
            order = _candidate_flat_index(SEL_TOK)
            rest = cand
            self = jnp.zeros(cand.shape, F32)
            for r in range(PEER_TOPK):
                _, hit = _extract_max_by_index(rest, order)
                self = jnp.where(hit, 1.0, self)
                rest = jnp.where(hit, -jnp.inf, rest)
            counts = _counts_per_first_rank(self)
            l1 = jnp.zeros(s1.shape, F32)
            for a in range(PEER_TOPK):
                l1 = jnp.where(rank1 == float(a), counts[a], l1)
            _write_selection(h, s1, s2, v1, v2, l1, rank2, self, cand, out_refs)
        return carry

    lax.fori_loop(0, PEER_HEADS, redo_if_tied, 0)


def peer_select(hT, wqT_bf, keys_bf):
    D, N = hT.shape
    assert N % SEL_TOK == 0
    H = PEER_HEADS
    blk = pl.BlockSpec((H, N_KEYS, SEL_TOK), lambda t: (0, 0, t))
    return pl.pallas_call(
        _peer_select_kernel,
        grid=(N // SEL_TOK,),
        in_specs=[
            pl.BlockSpec((D, SEL_TOK), lambda t: (0, t)),
            pl.BlockSpec((H * PEER_QDIM, D), lambda t: (0, 0)),
            pl.BlockSpec((2 * H, N_KEYS, D_HALF), lambda t: (0, 0, 0)),
        ],
        out_specs=[blk, blk, blk, blk],
        out_shape=[
            jax.ShapeDtypeStruct((H, N_KEYS, N), jnp.uint32),
            jax.ShapeDtypeStruct((H, N_KEYS, N), jnp.uint32),
            jax.ShapeDtypeStruct((H, N_KEYS, N), BF16),
            jax.ShapeDtypeStruct((H, N_KEYS, N), BF16),
        ],
        scratch_shapes=[
            pltpu.VMEM((H * PEER_QDIM, SEL_TOK), BF16),
            pltpu.VMEM((PEER_TOPK, SEL_TOK), F32),
            pltpu.VMEM((PEER_TOPK, SEL_TOK), F32),
            pltpu.VMEM((H, SEL_TOK), F32),
        ],
        compiler_params=pltpu.CompilerParams(
            dimension_semantics=("parallel",),
            vmem_limit_bytes=VMEM_LIMIT_BYTES),
        name="peer_select",
    )(hT, wqT_bf, keys_bf)


PEER_TOK = 512
PEER_EC = 1024
PEER_LANES = 256
BF16_ROWS = 16

LOG2_E = 1.4426950408889634
GELU_A = -2.0 * 0.7978845608028654 * LOG2_E
GELU_B = GELU_A * 0.044715


def _gelu_tanh(x):
    return x / (1.0 + jnp.exp2(x * (GELU_A + GELU_B * (x * x))))


N_CHUNKS = N_EXPERTS // PEER_EC
PIPE_DEPTH = 2


N_TILE_ROWS = PEER_EC // N_KEYS
N_LANE_GROUPS = PEER_TOK // PEER_LANES
MXU_K = 256
MXU_N = 256


def _peer_gate_rows(chunk, il, lane_groups, s_in, w_out, l1w_ref, e1w_ref, r2_ref, e2_ref):
    n_sub = N_KEYS // BF16_ROWS
    i = chunk * N_TILE_ROWS + il
    for lg in lane_groups:
        lanes = slice(lg * PEER_LANES, (lg + 1) * PEER_LANES)
        gs = [jnp.zeros((BF16_ROWS, PEER_LANES), BF16) for _ in range(n_sub)]
        for h in range(PEER_HEADS):
            l1 = pltpu.bitcast(jnp.broadcast_to(l1w_ref[h, pl.ds(i, 1), lanes], (8, PEER_LANES)), BF16)
            e1 = pltpu.bitcast(jnp.broadcast_to(e1w_ref[h, pl.ds(i, 1), lanes], (8, PEER_LANES)), BF16)
            for sub in range(n_sub):
                rows = slice(sub * BF16_ROWS, (sub + 1) * BF16_ROWS)
                gate = e2_ref[h, rows, lanes] * e1
                gs[sub] = gs[sub] + jnp.where(r2_ref[h, rows, lanes] < l1, gate, jnp.zeros_like(gate))
        for sub in range(n_sub):
            rows = slice(il * N_KEYS + sub * BF16_ROWS, il * N_KEYS + (sub + 1) * BF16_ROWS)
            act = _gelu_tanh(s_in[rows, lanes]).astype(BF16)
            w_out[rows, lanes] = gs[sub] * act


def _peer_kernel(xT_ref, u_ref, vT_ref, l1w_ref, e1w_ref, r2_ref, e2_ref, o_ref,
                 s_a, s_b, w_a, w_b, acc_scr, *, n_work):
    s = pl.program_id(0)

    @pl.when(s == 0)
    def _():
        s_a[...] = jnp.zeros_like(s_a)
        s_b[...] = jnp.zeros_like(s_b)
        w_a[...] = jnp.zeros_like(w_a)
        w_b[...] = jnp.zeros_like(w_b)

    drain = s - PIPE_DEPTH
    drain_chunk = drain % N_CHUNKS

    @pl.when((drain_chunk == 0) | (s == 0))
    def _():
        acc_scr[...] = jnp.zeros_like(acc_scr)

    gate_chunk = jnp.clip(s - 1, 0, n_work - 1) % N_CHUNKS

    def step(s_out, s_in, w_out, w_in):
        d_model = xT_ref.shape[0]
        lane_tiles = range(PEER_TOK // MXU_N)
        pieces1 = [(n, k) for n in lane_tiles for k in range(d_model // MXU_K)]
        pieces2 = [(n, k) for n in lane_tiles for k in range(PEER_EC // MXU_K)]
        assert len(pieces2) == N_TILE_ROWS and N_TILE_ROWS % len(pieces1) == 0
        every = N_TILE_ROWS // len(pieces1)
        gate_refs = (s_in, w_out, l1w_ref, e1w_ref, r2_ref, e2_ref)
        tile = lambda n, k: (slice(n * MXU_N, (n + 1) * MXU_N), slice(k * MXU_K, (k + 1) * MXU_K))
        for il in range(N_TILE_ROWS):
            part1 = None
            if il % every == 0:
                lanes1, ks1 = tile(*pieces1[il // every])
                part1 = jnp.dot(u_ref[:, ks1], xT_ref[ks1, lanes1], preferred_element_type=F32)
            lanes2, ks2 = tile(*pieces2[il])
            part2 = jnp.dot(vT_ref[:, ks2], w_in[ks2, lanes2], preferred_element_type=F32)
            _peer_gate_rows(gate_chunk, il, range(N_LANE_GROUPS), *gate_refs)
            if part1 is not None:
                if pieces1[il // every][1] == 0:
                    s_out[:, lanes1] = part1
                else:
                    s_out[:, lanes1] += part1
            acc_scr[:, lanes2] += part2

    @pl.when(s % 2 == 0)
    def _():
        step(s_a, s_b, w_b, w_a)

    @pl.when(s % 2 == 1)
    def _():
        step(s_b, s_a, w_a, w_b)

    @pl.when((drain >= 0) & (drain_chunk == N_CHUNKS - 1))
    def _():
        o_ref[...] = acc_scr[...]


def peer_dense(xT, u_bf, vT_bf, l1w, e1w, r2, e2):
    D, N = xT.shape
    assert N % PEER_TOK == 0
    H = PEER_HEADS
    n_work = (N // PEER_TOK) * N_CHUNKS
    fill = lambda s: jnp.minimum(s, n_work - 1)
    gate = lambda s: jnp.clip(s - 1, 0, n_work - 1)
    drain = lambda s: jnp.maximum(s - PIPE_DEPTH, 0)
    sel_blk = pl.BlockSpec((H, N_KEYS, PEER_TOK), lambda s: (0, 0, gate(s) // N_CHUNKS))
    return pl.pallas_call(
        functools.partial(_peer_kernel, n_work=n_work),
        grid=(n_work + PIPE_DEPTH,),
        in_specs=[
            pl.BlockSpec((D, PEER_TOK), lambda s: (0, fill(s) // N_CHUNKS)),
            pl.BlockSpec((PEER_EC, D), lambda s: (fill(s) % N_CHUNKS, 0)),
            pl.BlockSpec((D, PEER_EC), lambda s: (0, drain(s) % N_CHUNKS)),
            sel_blk, sel_blk, sel_blk, sel_blk,
        ],
        out_specs=pl.BlockSpec((D, PEER_TOK), lambda s: (0, drain(s) // N_CHUNKS)),
        out_shape=jax.ShapeDtypeStruct((D, N), F32),
        scratch_shapes=[
            pltpu.VMEM((PEER_EC, PEER_TOK), F32),
            pltpu.VMEM((PEER_EC, PEER_TOK), F32),
            pltpu.VMEM((PEER_EC, PEER_TOK), BF16),
            pltpu.VMEM((PEER_EC, PEER_TOK), BF16),
            pltpu.VMEM((D, PEER_TOK), F32),
        ],
        compiler_params=pltpu.CompilerParams(
            dimension_semantics=("arbitrary",),
            vmem_limit_bytes=VMEM_LIMIT_BYTES),
        name="peer_dense",
    )(xT, u_bf, vT_bf, l1w, e1w, r2, e2)


P_X = 0
P_Z = 1024
P_GP = 2048
P_GS = 3072
P_B = 4096
P_C = 4608
P_POOL = 5120
P_DT = 5632
P_COLS = 5760


def pack_w_in(w):
    pad = jnp.zeros((w.shape[0], LANES - 2 * SSD_HEADS), w.dtype)
    return jnp.concatenate([
        w[:, :COL_B], w[:, COL_Z:COL_POOL], w[:, COL_GP:COL_GS], w[:, COL_GS:],
        w[:, COL_B:COL_C], w[:, COL_C:COL_DTF], w[:, COL_POOL:COL_GP],
        w[:, COL_DTF:SCAN_COLS], pad], axis=1).astype(BF16)


def _sigmoid(x):
    return 1.0 / (1.0 + jnp.exp(-x))


CONV_TL = 512
CONV_TC = 512
HALO = 8


def _conv_kernel(cur_ref, prev_ref, next_ref, w_ref, b_ref, o_ref):
    j = pl.program_id(1)
    nj = pl.num_programs(1)
    prev = jnp.where(j == 0, 0.0, prev_ref[0, 0])
    nxt = jnp.where(j == nj - 1, 0.0, next_ref[0, 0])
    ext = jnp.concatenate([prev, cur_ref[0], nxt], axis=0)
    tl = cur_ref.shape[1]
    acc = jnp.zeros((tl, cur_ref.shape[2]), F32) + b_ref[...]
    for k in range(CONV_W):
        off = HALO + k - CONV_W // 2
        acc = acc + w_ref[k:k + 1, :] * ext[off:off + tl]
    o_ref[0] = acc * _sigmoid(acc)


def conv_silu(proj, conv_w8, conv_b):
    b, L, _ = proj.shape
    tl = min(CONV_TL, L)
    n_c = 2048 // CONV_TC
    proj4 = proj.reshape(b, L // HALO, HALO, P_COLS)
    nh = tl // HALO
    last_h = L // HALO - 1
    colmap = lambda c: jnp.where(c < 2, c, c + (P_B // CONV_TC - 2))
    return pl.pallas_call(
        _conv_kernel,
        grid=(b, L // tl, n_c),
        in_specs=[
            pl.BlockSpec((1, tl, CONV_TC), lambda i, j, c: (i, j, colmap(c))),
            pl.BlockSpec((1, 1, HALO, CONV_TC), lambda i, j, c: (i, jnp.maximum(j * nh - 1, 0), 0, colmap(c))),
            pl.BlockSpec((1, 1, HALO, CONV_TC), lambda i, j, c: (i, jnp.minimum((j + 1) * nh, last_h), 0, colmap(c))),
            pl.BlockSpec((8, CONV_TC), lambda i, j, c: (0, c)),
            pl.BlockSpec((1, CONV_TC), lambda i, j, c: (0, c)),
        ],
        out_specs=pl.BlockSpec((1, tl, CONV_TC), lambda i, j, c: (i, j, c)),
        out_shape=jax.ShapeDtypeStruct((b, L, 2048), F32),
        compiler_params=pltpu.CompilerParams(
            dimension_semantics=("parallel", "parallel", "parallel"),
            vmem_limit_bytes=VMEM_LIMIT_BYTES),
        name="conv_silu",
    )(proj, proj4, proj4, conv_w8, conv_b)


Q = SSD_CHUNK
XB_B = 1024
XB_C = 1536


def _split3(v):
    hi = v.astype(BF16)
    r1 = v - hi.astype(F32)
    mid = r1.astype(BF16)
    lo = (r1 - mid.astype(F32)).astype(BF16)
    return hi, mid, lo


SSD_PAIRS = SSD_HEADS // 2
PAIR_W = 2 * SSD_HEAD_DIM


def head_lane_selector():
    k = jnp.arange(2 * LANES)[:, None] % LANES
    col_head = jnp.arange(D_SSD)[None, :] // SSD_HEAD_DIM
    return jnp.stack([k == col_head, k == col_head + SSD_HEADS]).astype(BF16)


def _ssd_decays(dt_raw, a_lanes, bias_lanes, sel, rev):
    z = dt_raw + bias_lanes
    dt = jnp.maximum(z, 0.0) + jnp.log(1.0 + jnp.exp(-jnp.abs(z)))
    v = dt * (-jnp.exp(a_lanes))
    li = lax.broadcasted_iota(jnp.int32, (Q, Q), 0)
    si = lax.broadcasted_iota(jnp.int32, (Q, Q), 1)
    tri = (si >= li) if rev else (si <= li)
    tri_bf = jnp.where(tri, 1.0, 0.0).astype(BF16)
    hi, mid, lo = _split3(v)
    acum = (jnp.dot(tri_bf, hi, preferred_element_type=F32)
            + jnp.dot(tri_bf, mid, preferred_element_type=F32)
            + jnp.dot(tri_bf, lo, preferred_element_type=F32))
    last = 0 if rev else Q - 1
    tot = acum[last:last + 1, :]
    stack = jnp.concatenate([jnp.exp(acum), jnp.exp(tot - acum) * dt,
                             jnp.broadcast_to(jnp.exp(tot), (8, LANES))], axis=0)
    s_hi = stack.astype(BF16)
    s_mid = (stack - s_hi.astype(F32)).astype(BF16)
    spread = jnp.dot(jnp.concatenate([s_hi, s_mid], axis=1), sel, preferred_element_type=F32)
    return dict(tri=tri, acum=acum, acum_t=acum.T, dt_t=dt.T,
                e_in=spread[0:Q], w_end=spread[Q:2 * Q], e_tot=spread[2 * Q:2 * Q + 1],
                lane0=SSD_HEADS if rev else 0)


def _ssd_group(x_ref, g):
    bg = x_ref[0, :, XB_B + g * SSD_STATE:XB_B + (g + 1) * SSD_STATE]
    cg = x_ref[0, :, XB_C + g * SSD_STATE:XB_C + (g + 1) * SSD_STATE].astype(BF16)
    bg_t = bg.T.astype(BF16)
    cb = jnp.dot(cg, bg_t, preferred_element_type=F32)
    return cg, bg_t, cb


def _ssd_pair(d, grp, x_ref, h_scr, y_ref, pr):
    cg, bg_t, cb = grp
    cols = slice(pr * PAIR_W, (pr + 1) * PAIR_W)
    xh = x_ref[0, :, cols]
    xh_bf = xh.astype(BF16)
    halves = []
    for e in range(2):
        ln = d["lane0"] + 2 * pr + e
        seg = d["acum"][:, ln:ln + 1] - d["acum_t"][ln:ln + 1, :]
        decay = jnp.exp(jnp.where(d["tri"], seg, -jnp.inf))
        wgt = (cb * decay * d["dt_t"][ln:ln + 1, :]).astype(BF16)
        halves.append(jnp.dot(wgt, xh_bf, preferred_element_type=F32))
    first = lax.broadcasted_iota(jnp.int32, (Q, PAIR_W), 1) < SSD_HEAD_DIM
    h_t = h_scr[pr]
    y = (jnp.where(first, halves[0], halves[1])
         + d["e_in"][:, cols] * jnp.dot(cg, h_t.astype(BF16), preferred_element_type=F32))
    y_ref[0, :, cols] = y
    xs = (xh * d["w_end"][:, cols]).astype(BF16)
    h_scr[pr] = d["e_tot"][:, cols] * h_t + jnp.dot(bg_t, xs, preferred_element_type=F32)


def _ssd_kernel(xf_ref, dtf_ref, xb_ref, dtb_ref, a_ref, bias_ref, sel_ref, h0f_ref, h0b_ref,
                yf_ref, yb_ref, hf_ref, hb_ref, hf_scr, hb_scr):
    c = pl.program_id(1)

    @pl.when(c == 0)
    def _():
        hf_scr[...] = h0f_ref[0]
        hb_scr[...] = h0b_ref[0]

    df = _ssd_decays(dtf_ref[0], a_ref[...], bias_ref[...], sel_ref[0], False)
    db = _ssd_decays(dtb_ref[0], a_ref[...], bias_ref[...], sel_ref[1], True)
    pairs_per_group = SSD_HPG // 2
    for g in range(SSD_GROUPS):
        gf = _ssd_group(xf_ref, g)
        gb = _ssd_group(xb_ref, g)
        for r in range(pairs_per_group):
            pr = g * pairs_per_group + r
            _ssd_pair(df, gf, xf_ref, hf_scr, yf_ref, pr)
            _ssd_pair(db, gb, xb_ref, hb_scr, yb_ref, pr)

    @pl.when(c == pl.num_programs(1) - 1)
    def _():
        hf_ref[0] = hf_scr[...]
        hb_ref[0] = hb_scr[...]


def ssd_scan(xbc, proj, a_lanes, bias_lanes, sel, h0f, h0b):
    b, L, _ = xbc.shape
    nc = L // Q
    st_shape = (b, SSD_PAIRS, SSD_STATE, PAIR_W)
    st_spec = pl.BlockSpec((1, SSD_PAIRS, SSD_STATE, PAIR_W), lambda i, c: (i, 0, 0, 0))
    dt_blk = P_DT // LANES
    return pl.pallas_call(
        _ssd_kernel,
        grid=(b, nc),
        in_specs=[
            pl.BlockSpec((1, Q, 2048), lambda i, c: (i, c, 0)),
            pl.BlockSpec((1, Q, LANES), lambda i, c: (i, c, dt_blk)),
            pl.BlockSpec((1, Q, 2048), lambda i, c: (i, nc - 1 - c, 0)),
            pl.BlockSpec((1, Q, LANES), lambda i, c: (i, nc - 1 - c, dt_blk)),
            pl.BlockSpec((1, LANES), lambda i, c: (0, 0)),
            pl.BlockSpec((1, LANES), lambda i, c: (0, 0)),
            pl.BlockSpec((2, 2 * LANES, D_SSD), lambda i, c: (0, 0, 0)),
            st_spec, st_spec,
        ],
        out_specs=[
            pl.BlockSpec((1, Q, D_SSD), lambda i, c: (i, c, 0)),
            pl.BlockSpec((1, Q, D_SSD), lambda i, c: (i, nc - 1 - c, 0)),
            st_spec, st_spec,
        ],
        out_shape=[
            jax.ShapeDtypeStruct((b, L, D_SSD), F32),
            jax.ShapeDtypeStruct((b, L, D_SSD), F32),
            jax.ShapeDtypeStruct(st_shape, F32),
            jax.ShapeDtypeStruct(st_shape, F32),
        ],
        scratch_shapes=[
            pltpu.VMEM((SSD_PAIRS, SSD_STATE, PAIR_W), F32),
            pltpu.VMEM((SSD_PAIRS, SSD_STATE, PAIR_W), F32),
        ],
        compiler_params=pltpu.CompilerParams(
            dimension_semantics=("parallel", "arbitrary"),
            vmem_limit_bytes=VMEM_LIMIT_BYTES),
        name="ssd_scan",
    )(xbc, proj, xbc, proj, a_lanes, bias_lanes, sel, h0f, h0b)


POOL_TILE = 512


def _box_offsets(w):
    return w // 2, w - 1 - w // 2


def _pool_group(u_ref, o_ref, pad_scr, r_scr, w, L, row_len, two_d):
    lo, hi = _box_offsets(w)
    tile = min(POOL_TILE, L)
    n_tiles = L // tile
    half = (pad_scr.shape[0] - L) // 2

    if two_d:
        n_rows = L // row_len
        pad_scr[0:half, :] = jnp.zeros((half, LANES), F32)
        pad_scr[half + L:, :] = jnp.zeros((half, LANES), F32)
        pad_scr[half:half + L, :] = u_ref[0]

        def rows_body(i, carry):
            base = pl.multiple_of(i * tile, tile)
            acc = jnp.zeros((tile, LANES), F32)
            for k in range(-lo, hi + 1):
                acc = acc + pad_scr[pl.ds(base + (half + k * row_len), tile), :]
            row = (base + lax.broadcasted_iota(jnp.int32, (tile, LANES), 0)) // row_len
            cnt = jnp.minimum(row + hi, n_rows - 1) - jnp.maximum(row - lo, 0) + 1
            r_scr[pl.ds(base, tile), :] = acc / cnt.astype(F32)
            return carry

        lax.fori_loop(0, n_tiles, rows_body, 0)

    def cols_body(i, carry):
        base = pl.multiple_of(i * tile, tile)
        u = u_ref[0, pl.ds(base, tile), :]
        t = r_scr[pl.ds(base, tile), :] if two_d else u
        col = lax.broadcasted_iota(jnp.int32, (tile, LANES), 0) % row_len
        acc = t
        for k in range(-lo, hi + 1):
            if k == 0:
                continue
            sh = pltpu.roll(t, (-k) % tile, axis=0)
            ok = (col + k >= 0) & (col + k < row_len)
            acc = acc + jnp.where(ok, sh, 0.0)
        cnt = jnp.minimum(col + hi, row_len - 1) - jnp.maximum(col - lo, 0) + 1
        o_ref[0, pl.ds(base, tile), :] = acc / cnt.astype(F32) - u
        return carry

    lax.fori_loop(0, n_tiles, cols_body, 0)


def _pool_kernel(u_ref, o_ref, pad_scr, r_scr, *, L, row_len, two_d):
    g = pl.program_id(1)
    for gi, w in enumerate(POOL_WINDOWS):
        @pl.when(g == gi)
        def _(w=w):
            _pool_group(u_ref, o_ref, pad_scr, r_scr, w, L, row_len, two_d)


def pool_mixed(proj, two_d):
    b, L, _ = proj.shape
    row_len = GRID_W if two_d else L
    pad_rows = 2 * (max(POOL_WINDOWS) // 2) * row_len if two_d else 2 * HALO
    blk0 = P_POOL // LANES
    return pl.pallas_call(
        functools.partial(_pool_kernel, L=L, row_len=row_len, two_d=two_d),
        grid=(b, POOL_GROUPS),
        in_specs=[pl.BlockSpec((1, L, LANES), lambda i, g: (i, 0, blk0 + g))],
        out_specs=pl.BlockSpec((1, L, LANES), lambda i, g: (i, 0, g)),
        out_shape=jax.ShapeDtypeStruct((b, L, POOL_GROUPS * POOL_CH), F32),
        scratch_shapes=[
            pltpu.VMEM((L + pad_rows, LANES), F32),
            pltpu.VMEM((L, LANES), F32),
        ],
        compiler_params=pltpu.CompilerParams(
            dimension_semantics=("parallel", "parallel"),
            vmem_limit_bytes=VMEM_LIMIT_BYTES),
        name="pool_mixed",
    )(proj)


MIX_TM = 256
RMS_GROUP = D_SSD // SSD_GROUPS


def _layer_norm_rows(v, g, b):
    mu = jnp.mean(v, axis=-1, keepdims=True)
    d = v - mu
    var = jnp.mean(d * d, axis=-1, keepdims=True)
    return d * lax.rsqrt(var + LN_EPS) * g + b


def _mixer_kernel(yf_ref, yb_ref, xh_ref, z_ref, gp_ref, gs_ref, mixed_ref, x_ref,
                  dskip_ref, normg_ref, wssd_ref, poolw_ref, pscale_ref, wo_ref,
                  g1_ref, lng_ref, lnb_ref, sc2_ref, sh2_ref, xo_ref, hT_ref):
    y = yf_ref[0] + yb_ref[0] + dskip_ref[...] * xh_ref[0]
    z = z_ref[0]
    yg = y * (z * _sigmoid(z))
    parts = []
    for g in range(SSD_GROUPS):
        blk = yg[:, g * RMS_GROUP:(g + 1) * RMS_GROUP]
        ms = jnp.mean(blk * blk, axis=-1, keepdims=True)
        parts.append(blk * lax.rsqrt(ms + LN_EPS))
    yn = jnp.concatenate(parts, axis=-1) * normg_ref[...]
    y_s = jnp.dot(yn.astype(BF16), wssd_ref[...], preferred_element_type=F32)
    pooled = []
    for g in range(POOL_GROUPS):
        m = mixed_ref[0][:, g * POOL_CH:(g + 1) * POOL_CH].astype(BF16)
        pooled.append(jnp.dot(m, poolw_ref[g], preferred_element_type=F32))
    y_pool = jnp.concatenate(pooled, axis=-1) * pscale_ref[...]
    merged = _sigmoid(gp_ref[0]) * y_pool + _sigmoid(gs_ref[0]) * y_s
    mix = jnp.dot(merged.astype(BF16), wo_ref[...], preferred_element_type=F32)
    xn = _layer_norm_rows(DEEPNORM_ALPHA * x_ref[0] + g1_ref[0] * mix, lng_ref[...], lnb_ref[...])
    xo_ref[0] = xn
    h2 = xn * (1.0 + sc2_ref[0]) + sh2_ref[0]
    hT_ref[...] = h2.T.astype(BF16)


def mixer_epilogue(yf, yb, xbc, proj, mixed, x, dskip, normg, wssd_bf, poolw_bf, pscale, wo_bf,
                   g1, lng, lnb, sc2, sh2):
    b, L, D = x.shape
    tm = min(MIX_TM, L)
    nj = L // tm
    tok = lambda blk: pl.BlockSpec((1, tm, D), lambda i, j, blk=blk: (i, j, blk))
    vec = pl.BlockSpec((1, D), lambda i, j: (0, 0))
    bvec = pl.BlockSpec((1, 1, D), lambda i, j: (i, 0, 0))
    full2 = pl.BlockSpec((D, D), lambda i, j: (0, 0))
    return pl.pallas_call(
        _mixer_kernel,
        grid=(b, nj),
        in_specs=[
            tok(0), tok(0), tok(0),
            tok(P_Z // D), tok(P_GP // D), tok(P_GS // D),
            pl.BlockSpec((1, tm, POOL_GROUPS * POOL_CH), lambda i, j: (i, j, 0)),
            tok(0),
            vec, vec, full2,
            pl.BlockSpec((POOL_GROUPS, POOL_CH, D // POOL_GROUPS), lambda i, j: (0, 0, 0)),
            vec, full2,
            bvec, vec, vec, bvec, bvec,
        ],
        out_specs=[
            pl.BlockSpec((1, tm, D), lambda i, j: (i, j, 0)),
            pl.BlockSpec((D, tm), lambda i, j: (0, i * nj + j)),
        ],
        out_shape=[
            jax.ShapeDtypeStruct((b, L, D), F32),
            jax.ShapeDtypeStruct((D, b * L), BF16),
        ],
        compiler_params=pltpu.CompilerParams(
            dimension_semantics=("parallel", "parallel"),
            vmem_limit_bytes=VMEM_LIMIT_BYTES),
        name="mixer_epilogue",
    )(yf, yb, xbc, proj, proj, proj, mixed, x, dskip, normg, wssd_bf, poolw_bf, pscale, wo_bf,
      g1, lng, lnb, sc2, sh2)


def _peer_out_kernel(x_ref, peT_ref, g2_ref, lng_ref, lnb_ref, o_ref):
    v = DEEPNORM_ALPHA * x_ref[0] + g2_ref[0] * peT_ref[...].T
    o_ref[0] = _layer_norm_rows(v, lng_ref[...], lnb_ref[...])


def peer_epilogue(x, peT, g2, lng, lnb):
    b, L, D = x.shape
    tm = min(MIX_TM, L)
    nj = L // tm
    vec = pl.BlockSpec((1, D), lambda i, j: (0, 0))
    return pl.pallas_call(
        _peer_out_kernel,
        grid=(b, nj),
        in_specs=[
            pl.BlockSpec((1, tm, D), lambda i, j: (i, j, 0)),
            pl.BlockSpec((D, tm), lambda i, j: (0, i * nj + j)),
            pl.BlockSpec((1, 1, D), lambda i, j: (i, 0, 0)),
            vec, vec,
        ],
        out_specs=pl.BlockSpec((1, tm, D), lambda i, j: (i, j, 0)),
        out_shape=jax.ShapeDtypeStruct((b, L, D), F32),
        compiler_params=pltpu.CompilerParams(
            dimension_semantics=("parallel", "parallel"),
            vmem_limit_bytes=VMEM_LIMIT_BYTES),
        name="peer_epilogue",
    )(x, peT, g2, lng, lnb)


def _mixing_sublayer(xin, sc1, sh1, g1, sc2, sh2, h0f, h0b, two_d, w):
    proj = mod_matmul(xin, sc1, sh1, w["w_in"], tm=min(IN_PROJ_TM, xin.shape[1]))
    xbc = conv_silu(proj, w["conv_w8"], w["conv_b"])
    yf, yb, hf, hb = ssd_scan(xbc, proj, w["a_lanes"], w["bias_lanes"], w["sel"], h0f, h0b)
    mixed = pool_mixed(proj, two_d)
    xo, hT = mixer_epilogue(yf, yb, xbc, proj, mixed, xin, w["dskip"], w["normg"], w["w_ssd_out"],
                            w["pool_w"], w["pscale"], w["w_o"], g1, w["ln_g0"], w["ln_b0"], sc2, sh2)
    return xo, hT, hf, hb


def _peer_sublayer(xin, hT, g2, w):
    l1w, e1w, r2, e2 = peer_select(hT, w["wqT"], w["keys"])
    peT = peer_dense(hT, w["u"], w["vT"], l1w, e1w, r2, e2)
    return peer_epilogue(xin, peT, g2, w["ln_g1"], w["ln_b1"])


def kernel(x, c, ctx, c_ctx, w_mod, b_mod, w_in, conv_w, conv_b, a_log, dt_bias, d_skip,
           ssd_norm_g, w_ssd_out, pool_w, pool_scale, w_o, ln_g, ln_b, w_q, sub_keys, u_tab, v_tab):
    B = x.shape[0]
    silu_c = jax.nn.silu(c)
    silu_cc = jax.nn.silu(c_ctx)
    mod_in = jnp.pad(jnp.concatenate([silu_c, silu_cc[None]], axis=0), ((0, 8 - (B + 1)), (0, 0)))
    ones_b = jnp.ones((B, 1, 1), F32)
    pad_lanes = jnp.zeros((LANES - 2 * SSD_HEADS,), F32)
    state0 = jnp.zeros((B, SSD_PAIRS, SSD_STATE, PAIR_W), F32)
    sel = head_lane_selector()
    for l in range(DEPTH):
        last = l == DEPTH - 1
        mod_all = matmul(mod_in, w_mod[l].astype(BF16), tn=1536) + b_mod[l]
        sh1, sc1, g1, sh2, sc2, g2 = jnp.split(mod_all[:B, None, :], 6, axis=-1)
        mc = [m[None, None, :] * ones_b for m in jnp.split(mod_all[B], 6)]
        w = dict(
            w_in=pack_w_in(w_in[l]),
            conv_w8=jnp.pad(conv_w[l], ((0, 8 - CONV_W), (0, 0))),
            conv_b=conv_b[l][None],
            a_lanes=jnp.concatenate([a_log[l, 0], a_log[l, 1], pad_lanes])[None],
            bias_lanes=jnp.concatenate([dt_bias[l, 0], dt_bias[l, 1], pad_lanes])[None],
            sel=sel,
            dskip=jnp.repeat(d_skip[l], SSD_HEAD_DIM)[None],
            normg=ssd_norm_g[l][None],
            w_ssd_out=w_ssd_out[l].astype(BF16),
            pool_w=pool_w[l].astype(BF16),
            pscale=pool_scale[l][None],
            w_o=w_o[l].astype(BF16),
            ln_g0=ln_g[l, 0][None], ln_b0=ln_b[l, 0][None],
            ln_g1=ln_g[l, 1][None], ln_b1=ln_b[l, 1][None],
            wqT=w_q[l].T.astype(BF16),
            keys=sub_keys[l].reshape(2 * PEER_HEADS, N_KEYS, D_HALF).astype(BF16),
            u=u_tab[l].astype(BF16),
            vT=v_tab[l].T.astype(BF16),
        )
        ctx_mix, hT_c, hf_c, hb_c = _mixing_sublayer(ctx, mc[1], mc[0], mc[2], mc[4], mc[3],
                                                     state0, state0, False, w)
        x, hT, _, _ = _mixing_sublayer(x, sc1, sh1, g1, sc2, sh2, hf_c, hb_c, True, w)
        x = _peer_sublayer(x, hT, g2, w)
        if not last:
            ctx = _peer_sublayer(ctx_mix, hT_c, mc[5], w)
    return x
```
